```python
import jax, jax.numpy as jnp
from jax import lax
import numpy as np

D_MODEL = 1024
BATCH = 16
SEQ = 2048
DEPTH = 2
DEC_BATCH = 8
DEC_SEQ = 16
PAST_LEN = 4096

CHUNK = 64
N_META = 16
H_A = 4
DK = 128
DV = 128
W_A = H_A * DV
K_CONV = 4
QKV_DIM = 2 * H_A * DK + H_A * DV
W_B = D_MODEL - W_A
GROUP_CH = 16
G_B = W_B // GROUP_CH
P_STATE = 64
MIX_DIM = W_A + W_B
PROJ_DIM = QKV_DIM + W_A + 2 * H_A + W_B
D_FF = 2816
N_EXPERTS = 8
TOP_K = 2
D_FF_EXPERT = 1408
N_DENSE = (DEPTH + 1) // 2
N_MOE = DEPTH // 2
EPS = 1e-6

kernel_name = "hymba_gdn_s5_streaming_step"

F32 = jnp.float32


def rmsnorm(x, gain):
    xf = x.astype(F32)
    y = xf * lax.rsqrt(jnp.mean(xf * xf, axis=-1, keepdims=True) + EPS)
    return (y * gain.astype(F32)).astype(x.dtype)


def l2norm(x):
    return x * lax.rsqrt(jnp.sum(x * x, axis=-1, keepdims=True) + EPS)


def causal_conv(x_pre, conv_buf, w):
    xp = jnp.concatenate([conv_buf.astype(x_pre.dtype), x_pre], axis=1)
    y = lax.conv_general_dilated(xp, w.astype(x_pre.dtype)[:, None, :], window_strides=(1,), padding="VALID",
                                 dimension_numbers=("NWC", "WIO", "NWC"), feature_group_count=x_pre.shape[-1])
    return y, xp[:, -(K_CONV - 1):]


def delta_rule(q, k, v, g, beta, s0, chunk):
    b, t, h, _ = q.shape
    n = t // chunk

    def blk(z):
        return jnp.moveaxis(z.reshape((b, n, chunk, h) + z.shape[3:]), 3, 2)

    qc, kc, vc, gc, bc = blk(q), blk(k), blk(v), blk(g), blk(beta)
    G = jnp.cumsum(gc, axis=-1)
    idx = jnp.arange(chunk)
    incl = idx[:, None] >= idx[None, :]
    strict = idx[:, None] > idx[None, :]
    decay = jnp.exp(jnp.where(incl, G[..., :, None] - G[..., None, :], -jnp.inf))
    m = jnp.where(strict, bc[..., :, None] * jnp.einsum("bnhid,bnhjd->bnhij", kc, kc) * decay, 0.0)
    a = m + jnp.eye(chunk, dtype=m.dtype)
    eg = jnp.exp(G)[..., None]
    rhs = jnp.concatenate([bc[..., None] * vc, bc[..., None] * eg * kc], axis=-1)
    sol = lax.linalg.triangular_solve(a, rhs, left_side=True, lower=True, unit_diagonal=True)
    u0, w = sol[..., :DV], sol[..., DV:]
    qk = jnp.einsum("bnhid,bnhjd->bnhij", qc, kc) * decay
    qg = qc * eg
    g_last = G[..., -1]
    kg = kc * jnp.exp(g_last[..., None] - G)[..., None]

    def step(s, xs):
        u0_n, w_n, qk_n, qg_n, kg_n, gl_n = xs
        u = u0_n - jnp.einsum("bhik,bhkv->bhiv", w_n, s)
        o = jnp.einsum("bhik,bhkv->bhiv", qg_n, s) + jnp.einsum("bhij,bhjv->bhiv", qk_n, u)
        s = s * jnp.exp(gl_n)[..., None, None] + jnp.einsum("bhjk,bhjv->bhkv", kg_n, u)
        return s, o

    xs = tuple(jnp.moveaxis(z, 1, 0) for z in (u0, w, qk, qg, kg, g_last))
    s_fin, o = lax.scan(step, s0, xs)
    o = jnp.moveaxis(jnp.moveaxis(o, 0, 1), 3, 2).reshape(b, t, h, DV)
    return o, s_fin


def s5_layer(u, h_re, h_im, p):
    bsz, t, _ = u.shape
    lam_re = p["ssm_a_re"].astype(F32)
    lam_im = p["ssm_a_im"].astype(F32)
    delta = jnp.exp(p["ssm_log_dt"].astype(F32))[:, None]
    mag = jnp.exp(lam_re * delta)
    ab_re = mag * jnp.cos(lam_im * delta)
    ab_im = mag * jnp.sin(lam_im * delta)
    den = lam_re * lam_re + lam_im * lam_im
    f_re = ((ab_re - 1.0) * lam_re + ab_im * lam_im) / den
    f_im = (ab_im * lam_re - (ab_re - 1.0) * lam_im) / den
    b_re = p["ssm_b_re"].astype(F32)
    b_im = p["ssm_b_im"].astype(F32)
    bb_re = f_re[..., None] * b_re - f_im[..., None] * b_im
    bb_im = f_re[..., None] * b_im + f_im[..., None] * b_re
    ug = u.reshape(bsz, t, G_B, GROUP_CH)
    x_re = jnp.einsum("btgc,gpc->btgp", ug, bb_re)
    x_im = jnp.einsum("btgc,gpc->btgp", ug, bb_im)
    h_re = h_re.astype(F32)
    h_im = h_im.astype(F32)
    x_re = x_re.at[:, 0].add(ab_re * h_re - ab_im * h_im)
    x_im = x_im.at[:, 0].add(ab_re * h_im + ab_im * h_re)
    a_re = jnp.broadcast_to(ab_re, (1, t, G_B, P_STATE))
    a_im = jnp.broadcast_to(ab_im, (1, t, G_B, P_STATE))

    def combine(e1, e2):
        a1r, a1i, b1r, b1i = e1
        a2r, a2i, b2r, b2i = e2
        return (a2r * a1r - a2i * a1i, a2r * a1i + a2i * a1r,
                a2r * b1r - a2i * b1i + b2r, a2r * b1i + a2i * b1r + b2i)

    _, _, hr, hi = lax.associative_scan(combine, (a_re, a_im, x_re, x_im), axis=1)
    y = (jnp.einsum("btgp,gcp->btgc", hr, p["ssm_c_re"].astype(F32))
         - jnp.einsum("btgp,gcp->btgc", hi, p["ssm_c_im"].astype(F32)))
    y = y.reshape(bsz, t, W_B) + p["ssm_d"].astype(F32) * u
    gy = jax.nn.gelu(y)
    out = gy * jax.nn.sigmoid(gy @ p["w_glu"].astype(F32) + p["b_glu"].astype(F32))
    return out, hr[:, -1], hi[:, -1]


def token_mixer(hn, conv_buf, s_delta, h_re, h_im, segments, p):
    dt = hn.dtype
    bsz, t = hn.shape[:2]
    proj = hn @ p["w_in"]
    o1 = QKV_DIM
    o2 = o1 + W_A
    o3 = o2 + H_A
    o4 = o3 + H_A
    qkv_pre, z, b_raw, a_raw, u = proj[..., :o1], proj[..., o1:o2], proj[..., o2:o3], proj[..., o3:o4], proj[..., o4:]
    qkv, conv_new = causal_conv(qkv_pre, conv_buf, p["conv_w"])
    qkv = jax.nn.silu(qkv.astype(F32))
    q = l2norm(qkv[..., :H_A * DK].reshape(bsz, t, H_A, DK)) * (DK ** -0.5)
    k = l2norm(qkv[..., H_A * DK:2 * H_A * DK].reshape(bsz, t, H_A, DK))
    v = qkv[..., 2 * H_A * DK:].reshape(bsz, t, H_A, DV)
    beta = jax.nn.sigmoid(b_raw.astype(F32))
    g = -jnp.exp(p["a_log"].astype(F32)) * jax.nn.softplus(a_raw.astype(F32) + p["dt_bias"].astype(F32))
    s = s_delta.astype(F32)
    outs = []
    for start, stop, chunk in segments:
        o_seg, s = delta_rule(q[:, start:stop], k[:, start:stop], v[:, start:stop],
                              g[:, start:stop], beta[:, start:stop], s, chunk)
        outs.append(o_seg)
    o = jnp.concatenate(outs, axis=1)
    o = o * lax.rsqrt(jnp.mean(o * o, axis=-1, keepdims=True) + EPS) * p["norm_gate"].astype(F32)
    o = o * jax.nn.silu(z.astype(F32)).reshape(bsz, t, H_A, DV)
    out_a = o.reshape(bsz, t, W_A)
    out_b, r_new, i_new = s5_layer(u.astype(F32), h_re, h_im, p)
    out_b = out_b * lax.rsqrt(jnp.mean(out_b * out_b, axis=-1, keepdims=True) + EPS) * p["norm_ssm"].astype(F32)
    mixed = jnp.concatenate([out_a, out_b], axis=-1).astype(dt) @ p["w_out"]
    return mixed, conv_new.astype(dt), s.astype(dt), r_new.astype(dt), i_new.astype(dt)


def swiglu(x, wg, wu, wd):
    return (jax.nn.silu(x @ wg) * (x @ wu)) @ wd


def moe_ffn(x, router_w, wg, wu, wd):
    logits = (x @ router_w).astype(F32)
    top_v, top_i = lax.top_k(logits, TOP_K)
    gates = jax.nn.softmax(top_v, axis=-1)
    comb = jnp.sum(jax.nn.one_hot(top_i, N_EXPERTS, dtype=F32) * gates[..., None], axis=-2)
    out = jnp.zeros(x.shape, F32)
    for e in range(N_EXPERTS):
        out = out + comb[..., e:e + 1] * swiglu(x, wg[e], wu[e], wd[e]).astype(F32)
    return out.astype(x.dtype)


def trunk(x, conv_bufs, delta_states, ssm_re, ssm_im, segments, layer, ffn, norm_final):
    conv_out, delta_out, re_out, im_out = [], [], [], []
    for l in range(DEPTH):
        p = {name: arr[l] for name, arr in layer.items()}
        hn = rmsnorm(x, p["norm_mix"])
        m, c_new, d_new, r_new, i_new = token_mixer(hn, conv_bufs[l], delta_states[l], ssm_re[l], ssm_im[l], segments, p)
        x = x + m
        hn = rmsnorm(x, p["norm_ffn"])
        j = l // 2
        if l % 2 == 0:
            x = x + swiglu(hn, ffn["ffn_w_gate"][j], ffn["ffn_w_up"][j], ffn["ffn_w_down"][j])
        else:
            x = x + moe_ffn(hn, ffn["router_w"][j], ffn["moe_w_gate"][j], ffn["moe_w_up"][j], ffn["moe_w_down"][j])
        conv_out.append(c_new)
        delta_out.append(d_new)
        re_out.append(r_new)
        im_out.append(i_new)
    return (rmsnorm(x, norm_final), jnp.stack(delta_out), jnp.stack(conv_out), jnp.stack(re_out), jnp.stack(im_out))


def setup_inputs(seed: int = 0) -> dict:
    key = jax.random.key(seed)
    ks = iter(jax.random.split(key, 48))
    nrm = lambda shape, scale: jax.random.normal(next(ks), shape, F32) * scale
    uni = lambda shape, lo, hi: jax.random.uniform(next(ks), shape, F32, lo, hi)
    dt_init = jnp.exp(uni((DEPTH, H_A), float(np.log(1e-3)), float(np.log(1e-1))))
    n_idx = jnp.arange(P_STATE, dtype=F32)
    return {
        "x_prompt": nrm((BATCH, SEQ, D_MODEL), 1.0),
        "x_sample": nrm((DEC_BATCH, DEC_SEQ, D_MODEL), 1.0),
        "state_delta": nrm((DEPTH, DEC_BATCH, H_A, DK, DV), 0.1),
        "state_conv": nrm((DEPTH, DEC_BATCH, K_CONV - 1, QKV_DIM), 1.0),
        "state_ssm_re": nrm((DEPTH, DEC_BATCH, G_B, P_STATE), 0.1),
        "state_ssm_im": nrm((DEPTH, DEC_BATCH, G_B, P_STATE), 0.1),
        "meta_tokens": nrm((N_META, D_MODEL), 1.0),
        "norm_mix": 1.0 + nrm((DEPTH, D_MODEL), 0.02),
        "w_in": nrm((DEPTH, D_MODEL, PROJ_DIM), D_MODEL ** -0.5),
        "conv_w": nrm((DEPTH, K_CONV, QKV_DIM), K_CONV ** -0.5),
        "a_log": jnp.log(uni((DEPTH, H_A), 1.0, 16.0)),
        "dt_bias": jnp.log(jnp.expm1(dt_init)),
        "norm_gate": 1.0 + nrm((DEPTH, DV), 0.02),
        "ssm_a_re": -0.5 + nrm((DEPTH, G_B, P_STATE), 0.01),
        "ssm_a_im": np.pi * n_idx + nrm((DEPTH, G_B, P_STATE), 0.01),
        "ssm_b_re": nrm((DEPTH, G_B, P_STATE, GROUP_CH), (2 * GROUP_CH) ** -0.5),
        "ssm_b_im": nrm((DEPTH, G_B, P_STATE, GROUP_CH), (2 * GROUP_CH) ** -0.5),
        "ssm_c_re": nrm((DEPTH, G_B, GROUP_CH, P_STATE), (2 * P_STATE) ** -0.5),
        "ssm_c_im": nrm((DEPTH, G_B, GROUP_CH, P_STATE), (2 * P_STATE) ** -0.5),
        "ssm_d": nrm((DEPTH, W_B), 1.0),
        "ssm_log_dt": uni((DEPTH, G_B), float(np.log(1e-3)), float(np.log(1e-1))),
        "w_glu": nrm((DEPTH, W_B, W_B), W_B ** -0.5),
        "b_glu": nrm((DEPTH, W_B), 0.01),
        "norm_ssm": 1.0 + nrm((DEPTH, W_B), 0.02),
        "w_out": nrm((DEPTH, MIX_DIM, D_MODEL), MIX_DIM ** -0.5),
        "norm_ffn": 1.0 + nrm((DEPTH, D_MODEL), 0.02),
        "ffn_w_gate": nrm((N_DENSE, D_MODEL, D_FF), D_MODEL ** -0.5),
        "ffn_w_up": nrm((N_DENSE, D_MODEL, D_FF), D_MODEL ** -0.5),
        "ffn_w_down": nrm((N_DENSE, D_FF, D_MODEL), D_FF ** -0.5),
        "router_w": nrm((N_MOE, D_MODEL, N_EXPERTS), D_MODEL ** -0.5),
        "moe_w_gate": nrm((N_MOE, N_EXPERTS, D_MODEL, D_FF_EXPERT), D_MODEL ** -0.5),
        "moe_w_up": nrm((N_MOE, N_EXPERTS, D_MODEL, D_FF_EXPERT), D_MODEL ** -0.5),
        "moe_w_down": nrm((N_MOE, N_EXPERTS, D_FF_EXPERT, D_MODEL), D_FF_EXPERT ** -0.5),
        "norm_final": 1.0 + nrm((D_MODEL,), 0.02),
    }


def reference(x_prompt, x_sample, state_delta, state_conv, state_ssm_re, state_ssm_im,
              meta_tokens, norm_mix, w_in, conv_w, a_log, dt_bias, norm_gate,
              ssm_a_re, ssm_a_im, ssm_b_re, ssm_b_im, ssm_c_re, ssm_c_im, ssm_d, ssm_log_dt,
              w_glu, b_glu, norm_ssm, w_out, norm_ffn,
              ffn_w_gate, ffn_w_up, ffn_w_down, router_w, moe_w_gate, moe_w_up, moe_w_down,
              norm_final):
    layer = dict(norm_mix=norm_mix, w_in=w_in, conv_w=conv_w, a_log=a_log, dt_bias=dt_bias, norm_gate=norm_gate,
                 ssm_a_re=ssm_a_re, ssm_a_im=ssm_a_im, ssm_b_re=ssm_b_re, ssm_b_im=ssm_b_im,
                 ssm_c_re=ssm_c_re, ssm_c_im=ssm_c_im, ssm_d=ssm_d, ssm_log_dt=ssm_log_dt,
                 w_glu=w_glu, b_glu=b_glu, norm_ssm=norm_ssm, w_out=w_out, norm_ffn=norm_ffn)
    ffn = dict(ffn_w_gate=ffn_w_gate, ffn_w_up=ffn_w_up, ffn_w_down=ffn_w_down, router_w=router_w,
               moe_w_gate=moe_w_gate, moe_w_up=moe_w_up, moe_w_down=moe_w_down)
    dt = x_prompt.dtype
    bp, tp = x_prompt.shape[0], x_prompt.shape[1]
    meta = jnp.broadcast_to(meta_tokens.astype(dt)[None], (bp, N_META, D_MODEL))
    xp = jnp.concatenate([meta, x_prompt], axis=1)
    z_conv = jnp.zeros((DEPTH, bp, K_CONV - 1, QKV_DIM), dt)
    z_delta = jnp.zeros((DEPTH, bp, H_A, DK, DV), dt)
    z_ssm = jnp.zeros((DEPTH, bp, G_B, P_STATE), dt)
    seg_p = ((0, N_META, N_META), (N_META, N_META + tp, CHUNK))
    yp, pd, pc, pr, pi = trunk(xp, z_conv, z_delta, z_ssm, z_ssm, seg_p, layer, ffn, norm_final)
    y_prompt = yp[:, N_META:]
    ts = x_sample.shape[1]
    seg_s = ((0, ts, ts),)
    y_sample, sd, sc, sr, si = trunk(x_sample, state_conv, state_delta, state_ssm_re, state_ssm_im,
                                     seg_s, layer, ffn, norm_final)
    return (y_prompt, y_sample, pd, pc, pr, pi, sd, sc, sr, si)
```

```python
import functools

import jax
import jax.numpy as jnp
from jax import lax
from jax.experimental import pallas as pl
from jax.experimental.pallas import tpu as pltpu

F32 = jnp.float32
BF16 = jnp.bfloat16

D_MODEL = 1024
N_META = 16
H_A = 4
DK = 128
DV = 128
W_A = H_A * DV
K_CONV = 4
QKV_DIM = 2 * H_A * DK + H_A * DV
W_B = D_MODEL - W_A
GROUP_CH = 16
G_B = W_B // GROUP_CH
P_STATE = 64
SSM_STATE = G_B * P_STATE
D_FF = 2816
N_EXPERTS = 8
D_FF_EXPERT = 1408
EPS = 1e-6

LANES = 128
SUBLANES = 8
PROJ_PAD = QKV_DIM + W_A + W_B + LANES
COL_Z = QKV_DIM
COL_U = QKV_DIM + W_A
COL_BA = QKV_DIM + W_A + W_B
CONV_PAD = SUBLANES
SSM_HALF = W_B // 2
SSM_HALF_STATE = SSM_STATE // 2
MIX_BT = 8
CHUNK = 64
VMEM_LIMIT = 56 * 1024 * 1024


def _bdot(a, b):
    return jnp.dot(a.astype(BF16), b.astype(BF16), preferred_element_type=F32)


def _beinsum(spec, a, b):
    return jnp.einsum(spec, a.astype(BF16), b.astype(BF16), preferred_element_type=F32)


def _sigmoid(x):
    return 1.0 / (1.0 + jnp.exp(-x))


def _silu(x):
    return x * _sigmoid(x)


def _rms(x, gain):
    return x * lax.rsqrt(jnp.mean(x * x, axis=-1, keepdims=True) + EPS) * gain


def _resident(shape, grid_rank):
    zeros = (0,) * len(shape)
    index_map = (lambda i: zeros) if grid_rank == 1 else (lambda i, j: zeros)
    return pl.BlockSpec(shape, index_map, pipeline_mode=pl.Buffered(1))


def _in_proj_kernel(x_ref, g_ref, w_ref, o_ref):
    hn = _rms(x_ref[...], g_ref[...])
    o_ref[...] = _bdot(hn, w_ref[...])


def _in_proj(x, gain, w, tm):
    n = x.shape[0]
    return pl.pallas_call(
        _in_proj_kernel,
        grid=(pl.cdiv(n, tm),),
        in_specs=[pl.BlockSpec((tm, D_MODEL), lambda i: (i, 0)),
                  _resident((1, D_MODEL), 1),
                  _resident((D_MODEL, PROJ_PAD), 1)],
        out_specs=pl.BlockSpec((tm, PROJ_PAD), lambda i: (i, 0)),
        out_shape=jax.ShapeDtypeStruct((n, PROJ_PAD), F32),
        compiler_params=pltpu.CompilerParams(dimension_semantics=("parallel",),
                                             vmem_limit_bytes=VMEM_LIMIT),
        name="in_proj",
    )(x, gain, w)


def _delta_head(q, k, v, beta, gcum, s):
    bt, l, _ = q.shape
    g_hi = gcum.astype(BF16).astype(F32)
    r1 = gcum - g_hi
    g_mid = r1.astype(BF16).astype(F32)
    g_lo = r1 - g_mid
    lane = lax.broadcasted_iota(jnp.int32, (bt, l, LANES), 2)
    pieces = jnp.where(lane == 0, g_hi, jnp.where(lane == 1, g_mid, jnp.where(lane == 2, g_lo, 0.0)))
    ones = jnp.ones((bt, l, LANES), F32)
    g_row = _beinsum("bik,bjk->bij", ones, pieces)
    ii = lax.broadcasted_iota(jnp.int32, (bt, l, l), 1)
    jj = lax.broadcasted_iota(jnp.int32, (bt, l, l), 2)
    incl = ii >= jj
    strict = ii > jj
    decay = jnp.where(incl, jnp.exp(jnp.where(incl, gcum - g_row, 0.0)), 0.0)
    kk = _beinsum("bik,bjk->bij", k, k)
    qk = _beinsum("bik,bjk->bij", q, k) * decay
    neg_m = jnp.where(strict, -(beta * kk * decay), 0.0)
    eye = jnp.where(ii == jj, 1.0, 0.0)
    t_inv = eye + neg_m
    p = neg_m
    size = 2
    while size < l:
        p = _beinsum("bij,bjk->bik", p, p)
        t_inv = t_inv + _beinsum("bij,bjk->bik", t_inv, p)
        size *= 2
    eg = jnp.exp(gcum)
    rhs = jnp.concatenate([beta * v, (beta * eg) * k], axis=-1)
    sol = _beinsum("bij,bjd->bid", t_inv, rhs)
    u0 = sol[..., :DV]
    w = sol[..., DV:]
    u = u0 - _beinsum("bik,bkv->biv", w, s)
    o = _beinsum("bik,bkv->biv", q * eg, s) + _beinsum("bij,bjv->biv", qk, u)
    g_last = gcum[:, l - 1:l, :]
    kg = k * jnp.exp(g_last - gcum)
    s_new = s * jnp.exp(g_last) + _beinsum("bjk,bjv->bkv", kg, u)
    return o, s_new


def _mixer_kernel(qkv_ref, z_ref, u_ref, ba_ref, s0_ref, c0_ref, hr0_ref, hi0_ref,
                  convw_ref, alog_ref, dtb_ref, ngate_ref,
                  are_ref, aim_ref, bbre_ref, bbim_ref, ccre_ref, ccim_ref,
                  d_ref, wglu_ref, bglu_ref, nssm_ref,
                  out_ref, s_ref, c_ref, hr_ref, hi_ref,
                  xp_scr, utb_scr, xr_scr, xi_scr, ytb_scr, *, bt, l):
    rows = bt * l
    chunk = pl.program_id(1)

    @pl.when(chunk == 0)
    def _():
        s_ref[...] = s0_ref[...]
        hr_ref[...] = hr0_ref[...]
        hi_ref[...] = hi0_ref[...]
        xp_scr[:, CONV_PAD - (K_CONV - 1):CONV_PAD, :] = c0_ref[...]

    xp_scr[:, CONV_PAD:CONV_PAD + l, :] = qkv_ref[...]
    acc = None
    for j in range(K_CONV):
        off = CONV_PAD - (K_CONV - 1) + j
        term = xp_scr[:, off:off + l, :] * convw_ref[j:j + 1, :]
        acc = term if acc is None else acc + term
    tail = xp_scr[:, CONV_PAD + l - (K_CONV - 1):CONV_PAD + l, :]
    xp_scr[:, CONV_PAD - (K_CONV - 1):CONV_PAD, :] = tail
    c_ref[...] = tail
    qkv = _silu(acc)

    ba = ba_ref[...].reshape(rows, LANES)
    beta_all = _sigmoid(ba)
    sp_in = ba + dtb_ref[...]
    softplus = jnp.maximum(sp_in, 0.0) + jnp.log1p(jnp.exp(-jnp.abs(sp_in)))
    g_all = -jnp.exp(alog_ref[...]) * softplus
    t_idx = lax.broadcasted_iota(jnp.int32, (rows, LANES), 0) % l
    shift = 1
    while shift < l:
        g_all = g_all + jnp.where(t_idx >= shift, pltpu.roll(g_all, shift, 0), 0.0)
        shift *= 2
    beta_all = beta_all.reshape(bt, l, LANES)
    g_all = g_all.reshape(bt, l, LANES)

    z = z_ref[...]
    for h in range(H_A):
        qh = qkv[:, :, h * DK:(h + 1) * DK]
        kh = qkv[:, :, (H_A + h) * DK:(H_A + h + 1) * DK]
        vh = qkv[:, :, 2 * H_A * DK + h * DV:2 * H_A * DK + (h + 1) * DV]
        qh = qh * lax.rsqrt(jnp.sum(qh * qh, axis=-1, keepdims=True) + EPS) * (DK ** -0.5)
        kh = kh * lax.rsqrt(jnp.sum(kh * kh, axis=-1, keepdims=True) + EPS)
        o, s_new = _delta_head(qh, kh, vh, beta_all[:, :, h:h + 1], g_all[:, :, H_A + h:H_A + h + 1],
                               s_ref[:, h])
        s_ref[:, h] = s_new
        o = _rms(o, ngate_ref[...]) * _silu(z[:, :, h * DV:(h + 1) * DV])
        out_ref[:, :, h * DV:(h + 1) * DV] = o.astype(out_ref.dtype)

    u_bt = u_ref[...].reshape(rows, W_B)
    for c in range(W_B // LANES):
        ytb_scr[c] = u_bt[:, c * LANES:(c + 1) * LANES]
    for t in range(l):
        for c in range(W_B // LANES):
            utb_scr[t * bt:(t + 1) * bt, c * LANES:(c + 1) * LANES] = ytb_scr[c, pl.ds(t, bt, stride=l), :]
    u_tb = utb_scr[...]
    for half in range(2):
        uh = u_tb[:, half * SSM_HALF:(half + 1) * SSM_HALF]
        cols = slice(half * SSM_HALF_STATE, (half + 1) * SSM_HALF_STATE)
        xr_scr[:, cols] = _bdot(uh, bbre_ref[half])
        xi_scr[:, cols] = _bdot(uh, bbim_ref[half])

    for half in range(2):
        cols = slice(half * SSM_HALF_STATE, (half + 1) * SSM_HALF_STATE)
        a_re = jnp.broadcast_to(are_ref[:, cols], (bt, SSM_HALF_STATE))
        a_im = jnp.broadcast_to(aim_ref[:, cols], (bt, SSM_HALF_STATE))

        def step(t, carry):
            h_re, h_im = carry
            r = pl.multiple_of(t * bt, bt)
            n_re = a_re * h_re - a_im * h_im + xr_scr[pl.ds(r, bt), cols]
            n_im = a_re * h_im + a_im * h_re + xi_scr[pl.ds(r, bt), cols]
            xr_scr[pl.ds(r, bt), cols] = n_re
            xi_scr[pl.ds(r, bt), cols] = n_im
            return n_re, n_im

        h_re, h_im = lax.fori_loop(0, l, step, (hr_ref[:, cols], hi_ref[:, cols]))
        hr_ref[:, cols] = h_re
        hi_ref[:, cols] = h_im

    y_parts = []
    for half in range(2):
        cols = slice(half * SSM_HALF_STATE, (half + 1) * SSM_HALF_STATE)
        y_parts.append(_bdot(xr_scr[:, cols], ccre_ref[half]) - _bdot(xi_scr[:, cols], ccim_ref[half]))
    y = jnp.concatenate(y_parts, axis=-1) + d_ref[...] * u_tb
    gy = 0.5 * y * (1.0 + jnp.tanh(0.7978845608028654 * (y + 0.044715 * (y * y * y))))
    ob = gy * _sigmoid(_bdot(gy, wglu_ref[...]) + bglu_ref[...])
    ob = _rms(ob, nssm_ref[...])
    for c in range(W_B // LANES):
        ytb_scr[c] = ob[:, c * LANES:(c + 1) * LANES]
    for b in range(bt):
        for c in range(W_B // LANES):
            out_ref[b, :, W_A + c * LANES:W_A + (c + 1) * LANES] = \
                ytb_scr[c, pl.ds(b, l, stride=bt), :].astype(out_ref.dtype)


def _mixer(proj, s0, c0, hr0, hi0, p, l):
    b, t, _ = proj.shape
    bt = MIX_BT
    rows = bt * l
    grid = (b // bt, t // l)

    def tile(width, col_block):
        return pl.BlockSpec((bt, l, width), lambda i, n: (i, n, col_block))

    def state(shape):
        nd = len(shape)
        return pl.BlockSpec((bt,) + shape, lambda i, n: (i,) + (0,) * nd)

    def const(shape):
        return _resident(shape, 2)

    consts = [p["conv_w"], p["a_log_row"], p["dt_bias_row"], p["norm_gate"],
              p["a_re"], p["a_im"], p["bb_re"], p["bb_im"], p["cc_re"], p["cc_im"],
              p["ssm_d"], p["w_glu"], p["b_glu"], p["norm_ssm"]]
    state_shapes = [(H_A, DK, DV), (K_CONV - 1, QKV_DIM), (SSM_STATE,), (SSM_STATE,)]
    out_shapes = ([jax.ShapeDtypeStruct((b, t, D_MODEL), BF16)]
                  + [jax.ShapeDtypeStruct((b,) + s, F32) for s in state_shapes])
    return pl.pallas_call(
        functools.partial(_mixer_kernel, bt=bt, l=l),
        grid=grid,
        in_specs=([tile(QKV_DIM, 0), tile(W_A, COL_Z // W_A), tile(W_B, COL_U // W_B),
                   tile(LANES, COL_BA // LANES)]
                  + [state(s) for s in state_shapes]
                  + [const(c.shape) for c in consts]),
        out_specs=[tile(D_MODEL, 0)] + [state(s) for s in state_shapes],
        out_shape=out_shapes,
        scratch_shapes=[pltpu.VMEM((bt, CONV_PAD + l, QKV_DIM), F32),
                        pltpu.VMEM((rows, W_B), F32),
                        pltpu.VMEM((rows, SSM_STATE), F32),
                        pltpu.VMEM((rows, SSM_STATE), F32),
                        pltpu.VMEM((W_B // LANES, rows, LANES), F32)],
        compiler_params=pltpu.CompilerParams(dimension_semantics=("parallel", "arbitrary"),
                                             vmem_limit_bytes=VMEM_LIMIT),
        name="mixer",
    )(proj, proj, proj, proj, s0, c0, hr0, hi0, *consts)


def _out_ffn_kernel(mix_ref, x_ref, wo_ref, g_ref, wg_ref, wu_ref, wd_ref, o_ref):
    x1 = x_ref[...] + jnp.dot(mix_ref[...], wo_ref[...], preferred_element_type=F32)
    hn = _rms(x1, g_ref[...]).astype(BF16)
    acc = x1
    for c in range(D_FF // D_FF_EXPERT):
        cols = slice(c * D_FF_EXPERT, (c + 1) * D_FF_EXPERT)
        hidden = _silu(jnp.dot(hn, wg_ref[:, cols], preferred_element_type=F32)) \
            * jnp.dot(hn, wu_ref[:, cols], preferred_element_type=F32)
        acc = acc + jnp.dot(hidden.astype(BF16), wd_ref[cols, :], preferred_element_type=F32)
    o_ref[...] = acc


def _out_ffn(mixed, x, w_out, gain, wg, wu, wd, tm):
    n = x.shape[0]
    row = lambda w: pl.BlockSpec((tm, w), lambda i: (i, 0))
    const = lambda a: _resident(a.shape, 1)
    return pl.pallas_call(
        _out_ffn_kernel,
        grid=(pl.cdiv(n, tm),),
        in_specs=[row(D_MODEL), row(D_MODEL), const(w_out), const(gain), const(wg), const(wu), const(wd)],
        out_specs=row(D_MODEL),
        out_shape=jax.ShapeDtypeStruct((n, D_MODEL), F32),
        compiler_params=pltpu.CompilerParams(dimension_semantics=("parallel",),
                                             vmem_limit_bytes=VMEM_LIMIT),
        name="out_ffn",
    )(mixed, x, w_out, gain, wg, wu, wd)


def _out_router_kernel(mix_ref, x_ref, wo_ref, g_ref, wr_ref, x1_ref, hn_ref, comb_ref):
    x1 = x_ref[...] + jnp.dot(mix_ref[...], wo_ref[...], preferred_element_type=F32)
    x1_ref[...] = x1
    hn = _rms(x1, g_ref[...]).astype(BF16)
    hn_ref[...] = hn
    logits = jnp.dot(hn, wr_ref[...], preferred_element_type=F32)
    lane = lax.broadcasted_iota(jnp.int32, logits.shape, 1).astype(F32)
    neg = jnp.float32(-jnp.inf)
    logits = jnp.where(lane < N_EXPERTS, logits, neg)
    v1 = jnp.max(logits, axis=-1, keepdims=True)
    i1 = jnp.min(jnp.where(logits == v1, lane, float(LANES)), axis=-1, keepdims=True)
    rest = jnp.where(lane == i1, neg, logits)
    v2 = jnp.max(rest, axis=-1, keepdims=True)
    i2 = jnp.min(jnp.where(rest == v2, lane, float(LANES)), axis=-1, keepdims=True)
    e2 = jnp.exp(v2 - v1)
    den = 1.0 + e2
    comb_ref[...] = jnp.where(lane == i1, 1.0 / den, jnp.where(lane == i2, e2 / den, 0.0))


def _out_router(mixed, x, w_out, gain, w_router, tm):
    n = x.shape[0]
    row = lambda w: pl.BlockSpec((tm, w), lambda i: (i, 0))
    const = lambda a: _resident(a.shape, 1)
    return pl.pallas_call(
        _out_router_kernel,
        grid=(pl.cdiv(n, tm),),
        in_specs=[row(D_MODEL), row(D_MODEL), const(w_out), const(gain), const(w_router)],
        out_specs=[row(D_MODEL), row(D_MODEL), row(LANES)],
        out_shape=[jax.ShapeDtypeStruct((n, D_MODEL), F32),
                   jax.ShapeDtypeStruct((n, D_MODEL), BF16),
                   jax.ShapeDtypeStruct((n, LANES), F32)],
        compiler_params=pltpu.CompilerParams(dimension_semantics=("parallel",),
                                             vmem_limit_bytes=VMEM_LIMIT),
        name="out_router",
    )(mixed, x, w_out, gain, w_router)


def _moe_kernel(hn_ref, x1_ref, comb_ref, wg_ref, wu_ref, wd_ref, gf_ref, o_ref, acc_ref):
    e = pl.program_id(1)

    @pl.when(e == 0)
    def _():
        acc_ref[...] = jnp.zeros_like(acc_ref)

    hn = hn_ref[...]
    hidden = _silu(jnp.dot(hn, wg_ref[0], preferred_element_type=F32)) \
        * jnp.dot(hn, wu_ref[0], preferred_element_type=F32)
    y = jnp.dot(hidden.astype(BF16), wd_ref[0], preferred_element_type=F32)
    lane = lax.broadcasted_iota(jnp.int32, comb_ref.shape, 1)
    gate = jnp.sum(jnp.where(lane == e, comb_ref[...], 0.0), axis=-1, keepdims=True)
    acc_ref[...] += gate * y

    @pl.when(e == N_EXPERTS - 1)
    def _():
        o_ref[...] = _rms(x1_ref[...] + acc_ref[...], gf_ref[...])


def _moe(hn, x1, comb, wg, wu, wd, gain_final, tm):
    n = x1.shape[0]
    row = lambda w: pl.BlockSpec((tm, w), lambda i, e: (i, 0))
    return pl.pallas_call(
        _moe_kernel,
        grid=(pl.cdiv(n, tm), N_EXPERTS),
        in_specs=[row(D_MODEL), row(D_MODEL), row(LANES),
                  pl.BlockSpec((1, D_MODEL, D_FF_EXPERT), lambda i, e: (e, 0, 0)),
                  pl.BlockSpec((1, D_MODEL, D_FF_EXPERT), lambda i, e: (e, 0, 0)),
                  pl.BlockSpec((1, D_FF_EXPERT, D_MODEL), lambda i, e: (e, 0, 0)),
                  _resident((1, D_MODEL), 2)],
        out_specs=row(D_MODEL),
        out_shape=jax.ShapeDtypeStruct((n, D_MODEL), F32),
        scratch_shapes=[pltpu.VMEM((tm, D_MODEL), F32)],
        compiler_params=pltpu.CompilerParams(dimension_semantics=("parallel", "arbitrary"),
                                             vmem_limit_bytes=VMEM_LIMIT),
        name="moe",
    )(hn, x1, comb, wg, wu, wd, gain_final)


def _lane_row(vec, offset):
    return jnp.zeros((1, LANES), F32).at[0, offset:offset + vec.shape[0]].set(vec.astype(F32))


def _block_diag_halves(w):
    g, a, b = w.shape
    hg = g // 2
    w = w.reshape(2, hg, a, b)
    eye = jnp.eye(hg, dtype=w.dtype)
    return jnp.einsum("sgab,gh->sgahb", w, eye).reshape(2, hg * a, hg * b)


def _layer_params(l, a):
    w_in = a["w_in"][l]
    w_cat = jnp.concatenate([w_in[:, :QKV_DIM + W_A], w_in[:, QKV_DIM + W_A + 2 * H_A:],
                             w_in[:, QKV_DIM + W_A:QKV_DIM + W_A + 2 * H_A],
                             jnp.zeros((D_MODEL, LANES - 2 * H_A), F32)], axis=1)
    lam_re = a["ssm_a_re"][l]
    lam_im = a["ssm_a_im"][l]
    delta = jnp.exp(a["ssm_log_dt"][l])[:, None]
    mag = jnp.exp(lam_re * delta)
    ab_re = mag * jnp.cos(lam_im * delta)
    ab_im = mag * jnp.sin(lam_im * delta)
    den = lam_re * lam_re + lam_im * lam_im
    f_re = ((ab_re - 1.0) * lam_re + ab_im * lam_im) / den
    f_im = (ab_im * lam_re - (ab_re - 1.0) * lam_im) / den
    b_re = a["ssm_b_re"][l]
    b_im = a["ssm_b_im"][l]
    bb_re = f_re[..., None] * b_re - f_im[..., None] * b_im
    bb_im = f_re[..., None] * b_im + f_im[..., None] * b_re
    return dict(
        norm_mix=a["norm_mix"][l][None], w_in=w_cat.astype(BF16),
        conv_w=a["conv_w"][l],
        a_log_row=_lane_row(a["a_log"][l], H_A), dt_bias_row=_lane_row(a["dt_bias"][l], H_A),
        norm_gate=a["norm_gate"][l][None],
        a_re=ab_re.reshape(1, SSM_STATE), a_im=ab_im.reshape(1, SSM_STATE),
        bb_re=_block_diag_halves(jnp.swapaxes(bb_re, 1, 2)).astype(BF16),
        bb_im=_block_diag_halves(jnp.swapaxes(bb_im, 1, 2)).astype(BF16),
        cc_re=_block_diag_halves(jnp.swapaxes(a["ssm_c_re"][l], 1, 2)).astype(BF16),
        cc_im=_block_diag_halves(jnp.swapaxes(a["ssm_c_im"][l], 1, 2)).astype(BF16),
        ssm_d=a["ssm_d"][l][None], w_glu=a["w_glu"][l].astype(BF16), b_glu=a["b_glu"][l][None],
        norm_ssm=a["norm_ssm"][l][None],
        w_out=a["w_out"][l].astype(BF16), norm_ffn=a["norm_ffn"][l][None])


def _trunk(x, states, layers, ffn, l, tm):
    b, t, _ = x.shape
    n = b * t
    xf = x.reshape(n, D_MODEL)
    new_states = []
    for li, p in enumerate(layers):
        proj = _in_proj(xf, p["norm_mix"], p["w_in"], tm).reshape(b, t, PROJ_PAD)
        s0, c0, hr0, hi0 = states[li]
        mixed, s1, c1, hr1, hi1 = _mixer(proj, s0, c0, hr0.reshape(b, SSM_STATE),
                                         hi0.reshape(b, SSM_STATE), p, l)
        new_states.append((s1, c1, hr1.reshape(b, G_B, P_STATE), hi1.reshape(b, G_B, P_STATE)))
        mixed = mixed.reshape(n, D_MODEL)
        if li == 0:
            xf = _out_ffn(mixed, xf, p["w_out"], p["norm_ffn"], ffn["wg"], ffn["wu"], ffn["wd"], tm)
        else:
            x1, hn, comb = _out_router(mixed, xf, p["w_out"], p["norm_ffn"], ffn["router"], tm)
            xf = _moe(hn, x1, comb, ffn["moe_wg"], ffn["moe_wu"], ffn["moe_wd"], ffn["norm_final"], tm)
    return xf.reshape(b, t, D_MODEL), new_states


def kernel(x_prompt, x_sample, state_delta, state_conv, state_ssm_re, state_ssm_im, meta_tokens, norm_mix, w_in, conv_w, a_log, dt_bias, norm_gate, ssm_a_re, ssm_a_im, ssm_b_re, ssm_b_im, ssm_c_re, ssm_c_im, ssm_d, ssm_log_dt, w_glu, b_glu, norm_ssm, w_out, norm_ffn, ffn_w_gate, ffn_w_up, ffn_w_down, router_w, moe_w_gate, moe_w_up, moe_w_down, norm_final):
    a = dict(norm_mix=norm_mix, w_in=w_in, conv_w=conv_w, a_log=a_log, dt_bias=dt_bias, norm_gate=norm_gate,
             ssm_a_re=ssm_a_re, ssm_a_im=ssm_a_im, ssm_b_re=ssm_b_re, ssm_b_im=ssm_b_im,
             ssm_c_re=ssm_c_re, ssm_c_im=ssm_c_im, ssm_d=ssm_d, ssm_log_dt=ssm_log_dt,
             w_glu=w_glu, b_glu=b_glu, norm_ssm=norm_ssm, w_out=w_out, norm_ffn=norm_ffn)
    depth = w_in.shape[0]
    layers = [_layer_params(l, a) for l in range(depth)]
    ffn = dict(wg=ffn_w_gate[0].astype(BF16), wu=ffn_w_up[0].astype(BF16), wd=ffn_w_down[0].astype(BF16),
               router=jnp.pad(router_w[0], ((0, 0), (0, LANES - N_EXPERTS))).astype(BF16),
               moe_wg=moe_w_gate[0].astype(BF16), moe_wu=moe_w_up[0].astype(BF16),
               moe_wd=moe_w_down[0].astype(BF16), norm_final=norm_final[None])

    bp = x_prompt.shape[0]
    bs, ts = x_sample.shape[0], x_sample.shape[1]
    side_b = 2 * MIX_BT
    pad = side_b - bs - 1
    x_side = jnp.concatenate([x_sample, meta_tokens[None], jnp.zeros((pad, ts, D_MODEL), F32)], axis=0)

    def side_state(st):
        zeros = jnp.zeros((side_b - bs,) + st.shape[1:], F32)
        return jnp.concatenate([st, zeros], axis=0)

    side_states = [(side_state(state_delta[l]), side_state(state_conv[l]),
                    side_state(state_ssm_re[l]), side_state(state_ssm_im[l])) for l in range(depth)]
    y_side, side_new = _trunk(x_side, side_states, layers, ffn, ts, side_b * ts)

    def from_meta(st):
        return jnp.broadcast_to(st[bs:bs + 1], (bp,) + st.shape[1:])

    main_states = [tuple(from_meta(st) for st in side_new[l]) for l in range(depth)]
    y_prompt, main_new = _trunk(x_prompt, main_states, layers, ffn, CHUNK, 512)

    def stack(new, idx, count):
        return jnp.stack([new[l][idx][:count] for l in range(depth)])

    return (y_prompt, y_side[:bs],
            stack(main_new, 0, bp), stack(main_new, 1, bp), stack(main_new, 2, bp), stack(main_new, 3, bp),
            stack(side_new, 0, bs), stack(side_new, 1, bs), stack(side_new, 2, bs), stack(side_new, 3, bs))
```

```python
import functools

import jax
import jax.numpy as jnp
from jax import lax
from jax.experimental import pallas as pl
from jax.experimental.pallas import tpu as pltpu

F32 = jnp.float32
BF16 = jnp.bfloat16

D_MODEL = 1024
N_META = 16
H_A = 4
DK = 128
DV = 128
W_A = H_A * DV
K_CONV = 4
QKV_DIM = 2 * H_A * DK + H_A * DV
W_B = D_MODEL - W_A
GROUP_CH = 16
G_B = W_B // GROUP_CH
P_STATE = 64
SSM_STATE = G_B * P_STATE
D_FF = 2816
N_EXPERTS = 8
D_FF_EXPERT = 1408
EPS = 1e-6

LANES = 128
SUBLANES = 8
PROJ_PAD = QKV_DIM + W_A + W_B + LANES
COL_Z = QKV_DIM
COL_U = QKV_DIM + W_A
COL_BA = QKV_DIM + W_A + W_B
CONV_PAD = SUBLANES
SSM_HALF = W_B // 2
SSM_HALF_STATE = SSM_STATE // 2
MIX_BT = 8
CHUNK = 64
MOE_TM = 1024
MOE_SUB = LANES
MOE_CHUNK = 128
MOE_ALIGN = SUBLANES
VMEM_LIMIT = 56 * 1024 * 1024


class _Matmul:
    def __init__(self, precise):
        self.precise = precise
        self.act = F32 if precise else BF16

    def _args(self, a, b):
        if self.precise:
            return (a.astype(F32), b.astype(F32)), dict(precision=lax.Precision.HIGHEST)
        return (a.astype(BF16), b.astype(BF16)), {}

    def dot(self, a, b):
        args, kw = self._args(a, b)
        return jnp.dot(*args, preferred_element_type=F32, **kw)

    def einsum(self, spec, a, b):
        args, kw = self._args(a, b)
        return jnp.einsum(spec, *args, preferred_element_type=F32, **kw)


def _sigmoid(x):
    return 1.0 / (1.0 + jnp.exp(-x))


def _silu(x):
    return x * _sigmoid(x)


def _rms(x, gain):
    return x * lax.rsqrt(jnp.mean(x * x, axis=-1, keepdims=True) + EPS) * gain


def _resident(shape, grid_rank):
    zeros = (0,) * len(shape)
    index_map = (lambda i: zeros) if grid_rank == 1 else (lambda i, j: zeros)
    return pl.BlockSpec(shape, index_map, pipeline_mode=pl.Buffered(1))


def _in_proj_kernel(x_ref, g_ref, w_ref, o_ref, *, mm):
    hn = _rms(x_ref[...], g_ref[...])
    o_ref[...] = mm.dot(hn, w_ref[...])


def _in_proj(x, gain, w, tm, mm):
    n = x.shape[0]
    return pl.pallas_call(
        functools.partial(_in_proj_kernel, mm=mm),
        grid=(pl.cdiv(n, tm),),
        in_specs=[pl.BlockSpec((tm, D_MODEL), lambda i: (i, 0)),
                  _resident((1, D_MODEL), 1),
                  _resident((D_MODEL, PROJ_PAD), 1)],
        out_specs=pl.BlockSpec((tm, PROJ_PAD), lambda i: (i, 0)),
        out_shape=jax.ShapeDtypeStruct((n, PROJ_PAD), F32),
        compiler_params=pltpu.CompilerParams(dimension_semantics=("parallel",),
                                             vmem_limit_bytes=VMEM_LIMIT),
        name="in_proj",
    )(x, gain, w)


def _delta_head(q, k, v, beta, gcum, s, mm):
    bt, l, _ = q.shape
    g_hi = gcum.astype(BF16).astype(F32)
    r1 = gcum - g_hi
    g_mid = r1.astype(BF16).astype(F32)
    g_lo = r1 - g_mid
    lane = lax.broadcasted_iota(jnp.int32, (bt, l, LANES), 2)
    pieces = jnp.where(lane == 0, g_hi, jnp.where(lane == 1, g_mid, jnp.where(lane == 2, g_lo, 0.0)))
    ones = jnp.ones((bt, l, LANES), F32)
    g_row = mm.einsum("bik,bjk->bij", ones, pieces)
    ii = lax.broadcasted_iota(jnp.int32, (bt, l, l), 1)
    jj = lax.broadcasted_iota(jnp.int32, (bt, l, l), 2)
    incl = ii >= jj
    strict = ii > jj
    decay = jnp.where(incl, jnp.exp(jnp.where(incl, gcum - g_row, 0.0)), 0.0)
    kk = mm.einsum("bik,bjk->bij", k, k)
    qk = mm.einsum("bik,bjk->bij", q, k) * decay
    neg_m = jnp.where(strict, -(beta * kk * decay), 0.0)
    eye = jnp.where(ii == jj, 1.0, 0.0)
    t_inv = eye + neg_m
    p = neg_m
    size = 2
    while size < l:
        p = mm.einsum("bij,bjk->bik", p, p)
        t_inv = t_inv + mm.einsum("bij,bjk->bik", t_inv, p)
        size *= 2
    eg = jnp.exp(gcum)
    rhs = jnp.concatenate([beta * v, (beta * eg) * k], axis=-1)
    sol = mm.einsum("bij,bjd->bid", t_inv, rhs)
    u0 = sol[..., :DV]
    w = sol[..., DV:]
    u = u0 - mm.einsum("bik,bkv->biv", w, s)
    o = mm.einsum("bik,bkv->biv", q * eg, s) + mm.einsum("bij,bjv->biv", qk, u)
    g_last = gcum[:, l - 1:l, :]
    kg = k * jnp.exp(g_last - gcum)
    s_new = s * jnp.exp(g_last) + mm.einsum("bjk,bjv->bkv", kg, u)
    return o, s_new


def _mixer_kernel(qkv_ref, z_ref, u_ref, ba_ref, s0_ref, c0_ref, hr0_ref, hi0_ref,
                  convw_ref, alog_ref, dtb_ref, ngate_ref,
                  are_ref, aim_ref, bbre_ref, bbim_ref, ccre_ref, ccim_ref,
                  d_ref, wglu_ref, bglu_ref, nssm_ref,
                  out_ref, s_ref, c_ref, hr_ref, hi_ref,
                  xp_scr, utb_scr, xr_scr, xi_scr, ytb_scr, *, bt, l, mm):
    rows = bt * l
    chunk = pl.program_id(1)

    @pl.when(chunk == 0)
    def _():
        s_ref[...] = s0_ref[...]
        hr_ref[...] = hr0_ref[...]
        hi_ref[...] = hi0_ref[...]
        xp_scr[:, CONV_PAD - (K_CONV - 1):CONV_PAD, :] = c0_ref[...]

    xp_scr[:, CONV_PAD:CONV_PAD + l, :] = qkv_ref[...]
    acc = None
    for j in range(K_CONV):
        off = CONV_PAD - (K_CONV - 1) + j
        term = xp_scr[:, off:off + l, :] * convw_ref[j:j + 1, :]
        acc = term if acc is None else acc + term
    tail = xp_scr[:, CONV_PAD + l - (K_CONV - 1):CONV_PAD + l, :]
    xp_scr[:, CONV_PAD - (K_CONV - 1):CONV_PAD, :] = tail
    c_ref[...] = tail
    qkv = _silu(acc)

    ba = ba_ref[...].reshape(rows, LANES)
    beta_all = _sigmoid(ba)
    sp_in = ba + dtb_ref[...]
    softplus = jnp.maximum(sp_in, 0.0) + jnp.log1p(jnp.exp(-jnp.abs(sp_in)))
    g_all = -jnp.exp(alog_ref[...]) * softplus
    t_idx = lax.broadcasted_iota(jnp.int32, (rows, LANES), 0) % l
    shift = 1
    while shift < l:
        g_all = g_all + jnp.where(t_idx >= shift, pltpu.roll(g_all, shift, 0), 0.0)
        shift *= 2
    beta_all = beta_all.reshape(bt, l, LANES)
    g_all = g_all.reshape(bt, l, LANES)

    z = z_ref[...]
    for h in range(H_A):
        qh = qkv[:, :, h * DK:(h + 1) * DK]
        kh = qkv[:, :, (H_A + h) * DK:(H_A + h + 1) * DK]
        vh = qkv[:, :, 2 * H_A * DK + h * DV:2 * H_A * DK + (h + 1) * DV]
        qh = qh * lax.rsqrt(jnp.sum(qh * qh, axis=-1, keepdims=True) + EPS) * (DK ** -0.5)
        kh = kh * lax.rsqrt(jnp.sum(kh * kh, axis=-1, keepdims=True) + EPS)
        o, s_new = _delta_head(qh, kh, vh, beta_all[:, :, h:h + 1], g_all[:, :, H_A + h:H_A + h + 1],
                               s_ref[:, h], mm)
        s_ref[:, h] = s_new
        o = _rms(o, ngate_ref[...]) * _silu(z[:, :, h * DV:(h + 1) * DV])
        out_ref[:, :, h * DV:(h + 1) * DV] = o.astype(out_ref.dtype)

    u_bt = u_ref[...].reshape(rows, W_B)
    for c in range(W_B // LANES):
        ytb_scr[c] = u_bt[:, c * LANES:(c + 1) * LANES]
    for t in range(l):
        for c in range(W_B // LANES):
            utb_scr[t * bt:(t + 1) * bt, c * LANES:(c + 1) * LANES] = ytb_scr[c, pl.ds(t, bt, stride=l), :]
    u_tb = utb_scr[...]
    for half in range(2):
        uh = u_tb[:, half * SSM_HALF:(half + 1) * SSM_HALF]
        cols = slice(half * SSM_HALF_STATE, (half + 1) * SSM_HALF_STATE)
        xr_scr[:, cols] = mm.dot(uh, bbre_ref[half])
        xi_scr[:, cols] = mm.dot(uh, bbim_ref[half])

    for half in range(2):
        cols = slice(half * SSM_HALF_STATE, (half + 1) * SSM_HALF_STATE)
        a_re = jnp.broadcast_to(are_ref[:, cols], (bt, SSM_HALF_STATE))
        a_im = jnp.broadcast_to(aim_ref[:, cols], (bt, SSM_HALF_STATE))

        def step(t, carry):
            h_re, h_im = carry
            r = pl.multiple_of(t * bt, bt)
            n_re = a_re * h_re - a_im * h_im + xr_scr[pl.ds(r, bt), cols]
            n_im = a_re * h_im + a_im * h_re + xi_scr[pl.ds(r, bt), cols]
            xr_scr[pl.ds(r, bt), cols] = n_re
            xi_scr[pl.ds(r, bt), cols] = n_im
            return n_re, n_im

        h_re, h_im = lax.fori_loop(0, l, step, (hr_ref[:, cols], hi_ref[:, cols]))
        hr_ref[:, cols] = h_re
        hi_ref[:, cols] = h_im

    y_parts = []
    for half in range(2):
        cols = slice(half * SSM_HALF_STATE, (half + 1) * SSM_HALF_STATE)
        y_parts.append(mm.dot(xr_scr[:, cols], ccre_ref[half]) - mm.dot(xi_scr[:, cols], ccim_ref[half]))
    y = jnp.concatenate(y_parts, axis=-1) + d_ref[...] * u_tb
    gy = 0.5 * y * (1.0 + jnp.tanh(0.7978845608028654 * (y + 0.044715 * (y * y * y))))
    ob = gy * _sigmoid(mm.dot(gy, wglu_ref[...]) + bglu_ref[...])
    ob = _rms(ob, nssm_ref[...])
    for c in range(W_B // LANES):
        ytb_scr[c] = ob[:, c * LANES:(c + 1) * LANES]
    for b in range(bt):
        for c in range(W_B // LANES):
            out_ref[b, :, W_A + c * LANES:W_A + (c + 1) * LANES] = \
                ytb_scr[c, pl.ds(b, l, stride=bt), :].astype(out_ref.dtype)


def _mixer(proj, s0, c0, hr0, hi0, p, l, mm):
    b, t, _ = proj.shape
    bt = MIX_BT
    rows = bt * l
    grid = (b // bt, t // l)

    def tile(width, col_block):
        return pl.BlockSpec((bt, l, width), lambda i, n: (i, n, col_block))

    def state(shape):
        nd = len(shape)
        return pl.BlockSpec((bt,) + shape, lambda i, n: (i,) + (0,) * nd)

    def const(shape):
        return _resident(shape, 2)

    consts = [p["conv_w"], p["a_log_row"], p["dt_bias_row"], p["norm_gate"],
              p["a_re"], p["a_im"], p["bb_re"], p["bb_im"], p["cc_re"], p["cc_im"],
              p["ssm_d"], p["w_glu"], p["b_glu"], p["norm_ssm"]]
    state_shapes = [(H_A, DK, DV), (K_CONV - 1, QKV_DIM), (SSM_STATE,), (SSM_STATE,)]
    out_shapes = ([jax.ShapeDtypeStruct((b, t, D_MODEL), mm.act)]
                  + [jax.ShapeDtypeStruct((b,) + s, F32) for s in state_shapes])
    return pl.pallas_call(
        functools.partial(_mixer_kernel, bt=bt, l=l, mm=mm),
        grid=grid,
        in_specs=([tile(QKV_DIM, 0), tile(W_A, COL_Z // W_A), tile(W_B, COL_U // W_B),
                   tile(LANES, COL_BA // LANES)]
                  + [state(s) for s in state_shapes]
                  + [const(c.shape) for c in consts]),
        out_specs=[tile(D_MODEL, 0)] + [state(s) for s in state_shapes],
        out_shape=out_shapes,
        scratch_shapes=[pltpu.VMEM((bt, CONV_PAD + l, QKV_DIM), F32),
                        pltpu.VMEM((rows, W_B), F32),
                        pltpu.VMEM((rows, SSM_STATE), F32),
                        pltpu.VMEM((rows, SSM_STATE), F32),
                        pltpu.VMEM((W_B // LANES, rows, LANES), F32)],
        compiler_params=pltpu.CompilerParams(dimension_semantics=("parallel", "arbitrary"),
                                             vmem_limit_bytes=VMEM_LIMIT),
        name="mixer",
    )(proj, proj, proj, proj, s0, c0, hr0, hi0, *consts)


def _out_ffn_kernel(mix_ref, x_ref, wo_ref, g_ref, wg_ref, wu_ref, wd_ref, o_ref, *, mm):
    x1 = x_ref[...] + mm.dot(mix_ref[...], wo_ref[...])
    hn = _rms(x1, g_ref[...]).astype(mm.act)
    acc = x1
    for c in range(D_FF // D_FF_EXPERT):
        cols = slice(c * D_FF_EXPERT, (c + 1) * D_FF_EXPERT)
        hidden = _silu(mm.dot(hn, wg_ref[:, cols])) * mm.dot(hn, wu_ref[:, cols])
        acc = acc + mm.dot(hidden, wd_ref[cols, :])
    o_ref[...] = acc


def _out_ffn(mixed, x, w_out, gain, wg, wu, wd, tm, mm):
    n = x.shape[0]
    row = lambda w: pl.BlockSpec((tm, w), lambda i: (i, 0))
    const = lambda a: _resident(a.shape, 1)
    return pl.pallas_call(
        functools.partial(_out_ffn_kernel, mm=mm),
        grid=(pl.cdiv(n, tm),),
        in_specs=[row(D_MODEL), row(D_MODEL), const(w_out), const(gain), const(wg), const(wu), const(wd)],
        out_specs=row(D_MODEL),
        out_shape=jax.ShapeDtypeStruct((n, D_MODEL), F32),
        compiler_params=pltpu.CompilerParams(dimension_semantics=("parallel",),
                                             vmem_limit_bytes=VMEM_LIMIT),
        name="out_ffn",
    )(mixed, x, w_out, gain, wg, wu, wd)


def _out_router_kernel(mix_ref, x_ref, wo_ref, g_ref, wr_ref, x1_ref, hn_ref, comb_ref, combt_ref, *, mm):
    x1 = x_ref[...] + mm.dot(mix_ref[...], wo_ref[...])
    x1_ref[...] = x1
    hn = _rms(x1, g_ref[...]).astype(mm.act)
    hn_ref[...] = hn
    logits = mm.dot(hn, wr_ref[...])
    lane = lax.broadcasted_iota(jnp.int32, logits.shape, 1).astype(F32)
    neg = jnp.float32(-jnp.inf)
    logits = jnp.where(lane < N_EXPERTS, logits, neg)
    v1 = jnp.max(logits, axis=-1, keepdims=True)
    i1 = jnp.min(jnp.where(logits == v1, lane, float(LANES)), axis=-1, keepdims=True)
    rest = jnp.where(lane == i1, neg, logits)
    v2 = jnp.max(rest, axis=-1, keepdims=True)
    i2 = jnp.min(jnp.where(rest == v2, lane, float(LANES)), axis=-1, keepdims=True)
    e2 = jnp.exp(v2 - v1)
    den = 1.0 + e2
    comb = jnp.where(lane == i1, 1.0 / den, jnp.where(lane == i2, e2 / den, 0.0))
    comb_ref[...] = comb
    combt_ref[...] = comb.T[:N_EXPERTS, :]


def _out_router(mixed, x, w_out, gain, w_router, tm, mm):
    n = x.shape[0]
    row = lambda w: pl.BlockSpec((tm, w), lambda i: (i, 0))
    const = lambda a: _resident(a.shape, 1)
    return pl.pallas_call(
        functools.partial(_out_router_kernel, mm=mm),
        grid=(pl.cdiv(n, tm),),
        in_specs=[row(D_MODEL), row(D_MODEL), const(w_out), const(gain), const(w_router)],
        out_specs=[row(D_MODEL), row(D_MODEL), row(LANES),
                   pl.BlockSpec((N_EXPERTS, tm), lambda i: (0, i))],
        out_shape=[jax.ShapeDtypeStruct((n, D_MODEL), F32),
                   jax.ShapeDtypeStruct((n, D_MODEL), mm.act),
                   jax.ShapeDtypeStruct((n, LANES), F32),
                   jax.ShapeDtypeStruct((N_EXPERTS, n), F32)],
        compiler_params=pltpu.CompilerParams(dimension_semantics=("parallel",),
                                             vmem_limit_bytes=VMEM_LIMIT),
        name="out_router",
    )(mixed, x, w_out, gain, w_router)


def _moe_kernel(cnt_ref, hn_ref, x1_ref, comb_ref, combt_ref, wg_ref, wu_ref, wd_ref, gf_ref, o_ref,
                acc_ref, xc_ref, y_ref, *, tm, mm):
    i = pl.program_id(0)
    e = pl.program_id(1)
    nsub = tm // MOE_SUB

    @pl.when((i == 0) & (e == 0))
    def _():
        xc_ref[...] = jnp.zeros_like(xc_ref)
        y_ref[...] = jnp.zeros_like(y_ref)

    @pl.when(e == 0)
    def _():
        acc_ref[...] = jnp.zeros_like(acc_ref)

    r_i = lax.broadcasted_iota(jnp.int32, (MOE_SUB, MOE_SUB), 0)
    c_i = lax.broadcasted_iota(jnp.int32, (MOE_SUB, MOE_SUB), 1)
    before_row = jnp.where(c_i < r_i, 1.0, 0.0).astype(BF16)
    before_col = jnp.where(r_i < c_i, 1.0, 0.0).astype(BF16)
    lane = lax.broadcasted_iota(jnp.int32, (MOE_SUB, LANES), 1)

    spans = []
    base = 0
    for s in range(nsub):
        cnt = cnt_ref[(i * nsub + s) * N_EXPERTS + e]
        spans.append(base)
        base = base + ((cnt + MOE_ALIGN - 1) // MOE_ALIGN) * MOE_ALIGN
    total = base

    slot_row = lax.broadcasted_iota(jnp.int32, (MOE_SUB, MOE_SUB), 0).astype(F32)
    for s, base_s in enumerate(spans):
        rows = slice(s * MOE_SUB, (s + 1) * MOE_SUB)
        gates_t = combt_ref[:, rows]
        expert_row = lax.broadcasted_iota(jnp.int32, gates_t.shape, 0) == e
        routed = jnp.sum(jnp.where(expert_row, gates_t, 0.0), axis=0, keepdims=True) > 0.0
        rank = jnp.dot(jnp.broadcast_to(jnp.where(routed, 1.0, 0.0), (SUBLANES, MOE_SUB)).astype(BF16),
                       before_col, preferred_element_type=F32)[0:1]
        onehot = jnp.where((rank == slot_row) & routed, 1.0, 0.0).astype(BF16)
        dst = pl.ds(pl.multiple_of(base_s, MOE_ALIGN), MOE_SUB)
        xc_ref[dst, :] = mm.dot(onehot, hn_ref[rows, :])

    def ffn(c, carry):
        r = pl.ds(pl.multiple_of(c * MOE_CHUNK, MOE_CHUNK), MOE_CHUNK)
        x = xc_ref[r, :].astype(mm.act)
        hidden = _silu(mm.dot(x, wg_ref[0])) * mm.dot(x, wu_ref[0])
        y_ref[r, :] = mm.dot(hidden, wd_ref[0])
        return carry

    lax.fori_loop(0, (total + MOE_CHUNK - 1) // MOE_CHUNK, ffn, 0)

    pieces = 1 if mm.precise else 2
    slot_lane = lax.broadcasted_iota(jnp.int32, (MOE_SUB, pieces * MOE_SUB), 1)
    slot_lane = jnp.where(slot_lane >= MOE_SUB, slot_lane - MOE_SUB, slot_lane).astype(F32)
    for s, base_s in enumerate(spans):
        rows = slice(s * MOE_SUB, (s + 1) * MOE_SUB)
        gate = jnp.sum(jnp.where(lane == e, comb_ref[rows, :], 0.0), axis=-1, keepdims=True)
        routed = gate > 0.0
        rank = jnp.dot(before_row, jnp.broadcast_to(jnp.where(routed, 1.0, 0.0), (MOE_SUB, LANES)).astype(BF16),
                       preferred_element_type=F32)[:, 0:1]
        y = y_ref[pl.ds(pl.multiple_of(base_s, MOE_ALIGN), MOE_SUB), :]
        if not mm.precise:
            y_hi = y.astype(BF16)
            y = jnp.concatenate([y_hi, (y - y_hi.astype(F32)).astype(BF16)], axis=0)
        onehot = jnp.where((rank == slot_lane) & routed, 1.0, 0.0)
        back = mm.dot(onehot, y)
        acc_ref[rows, :] += gate * back

    @pl.when(e == N_EXPERTS - 1)
    def _():
        o_ref[...] = _rms(x1_ref[...] + acc_ref[...], gf_ref[...])


def _moe(hn, x1, comb, comb_t, wg, wu, wd, gain_final, tm, mm):
    n = x1.shape[0]
    assert n % tm == 0 and tm % MOE_SUB == 0
    routed = comb[:, :N_EXPERTS] > 0.0
    counts = jnp.sum(routed.reshape(n // MOE_SUB, MOE_SUB, N_EXPERTS), axis=1, dtype=jnp.int32).reshape(-1)
    row = lambda w: pl.BlockSpec((tm, w), lambda i, e, cnt: (i, 0))
    expert = lambda a, b: pl.BlockSpec((1, a, b), lambda i, e, cnt: (e, 0, 0))
    packed_rows = tm + max(MOE_CHUNK, MOE_SUB) + (tm // MOE_SUB) * MOE_ALIGN
    return pl.pallas_call(
        functools.partial(_moe_kernel, tm=tm, mm=mm),
        grid_spec=pltpu.PrefetchScalarGridSpec(
            num_scalar_prefetch=1,
            grid=(n // tm, N_EXPERTS),
            in_specs=[row(D_MODEL), row(D_MODEL), row(LANES),
                      pl.BlockSpec((N_EXPERTS, tm), lambda i, e, cnt: (0, i)),
                      expert(D_MODEL, D_FF_EXPERT), expert(D_MODEL, D_FF_EXPERT), expert(D_FF_EXPERT, D_MODEL),
                      pl.BlockSpec((1, D_MODEL), lambda i, e, cnt: (0, 0))],
            out_specs=row(D_MODEL),
            scratch_shapes=[pltpu.VMEM((tm, D_MODEL), F32),
                            pltpu.VMEM((packed_rows, D_MODEL), F32),
                            pltpu.VMEM((packed_rows, D_MODEL), F32)]),
        out_shape=jax.ShapeDtypeStruct((n, D_MODEL), F32),
        compiler_params=pltpu.CompilerParams(dimension_semantics=("arbitrary", "arbitrary"),
                                             vmem_limit_bytes=VMEM_LIMIT),
        name="moe",
    )(counts, hn, x1, comb, comb_t, wg, wu, wd, gain_final)


def _lane_row(vec, offset):
    return jnp.zeros((1, LANES), F32).at[0, offset:offset + vec.shape[0]].set(vec.astype(F32))


def _block_diag_halves(w):
    g, a, b = w.shape
    hg = g // 2
    w = w.reshape(2, hg, a, b)
    eye = jnp.eye(hg, dtype=w.dtype)
    return jnp.einsum("sgab,gh->sgahb", w, eye).reshape(2, hg * a, hg * b)


def _layer_params(l, a, wdtype):
    w_in = a["w_in"][l]
    w_cat = jnp.concatenate([w_in[:, :QKV_DIM + W_A], w_in[:, QKV_DIM + W_A + 2 * H_A:],
                             w_in[:, QKV_DIM + W_A:QKV_DIM + W_A + 2 * H_A],
                             jnp.zeros((D_MODEL, LANES - 2 * H_A), F32)], axis=1)
    lam_re = a["ssm_a_re"][l]
    lam_im = a["ssm_a_im"][l]
    delta = jnp.exp(a["ssm_log_dt"][l])[:, None]
    mag = jnp.exp(lam_re * delta)
    ab_re = mag * jnp.cos(lam_im * delta)
    ab_im = mag * jnp.sin(lam_im * delta)
    den = lam_re * lam_re + lam_im * lam_im
    f_re = ((ab_re - 1.0) * lam_re + ab_im * lam_im) / den
    f_im = (ab_im * lam_re - (ab_re - 1.0) * lam_im) / den
    b_re = a["ssm_b_re"][l]
    b_im = a["ssm_b_im"][l]
    bb_re = f_re[..., None] * b_re - f_im[..., None] * b_im
    bb_im = f_re[..., None] * b_im + f_im[..., None] * b_re
    return dict(
        norm_mix=a["norm_mix"][l][None], w_in=w_cat.astype(wdtype),
        conv_w=a["conv_w"][l],
        a_log_row=_lane_row(a["a_log"][l], H_A), dt_bias_row=_lane_row(a["dt_bias"][l], H_A),
        norm_gate=a["norm_gate"][l][None],
        a_re=ab_re.reshape(1, SSM_STATE), a_im=ab_im.reshape(1, SSM_STATE),
        bb_re=_block_diag_halves(jnp.swapaxes(bb_re, 1, 2)).astype(wdtype),
        bb_im=_block_diag_halves(jnp.swapaxes(bb_im, 1, 2)).astype(wdtype),
        cc_re=_block_diag_halves(jnp.swapaxes(a["ssm_c_re"][l], 1, 2)).astype(wdtype),
        cc_im=_block_diag_halves(jnp.swapaxes(a["ssm_c_im"][l], 1, 2)).astype(wdtype),
        ssm_d=a["ssm_d"][l][None], w_glu=a["w_glu"][l].astype(wdtype), b_glu=a["b_glu"][l][None],
        norm_ssm=a["norm_ssm"][l][None],
        w_out=a["w_out"][l].astype(wdtype), norm_ffn=a["norm_ffn"][l][None])


def _trunk(x, states, layers, ffn, l, tm, mm):
    b, t, _ = x.shape
    n = b * t
    xf = x.reshape(n, D_MODEL)
    new_states = []
    for li, p in enumerate(layers):
        proj = _in_proj(xf, p["norm_mix"], p["w_in"], tm, mm).reshape(b, t, PROJ_PAD)
        s0, c0, hr0, hi0 = states[li]
        mixed, s1, c1, hr1, hi1 = _mixer(proj, s0, c0, hr0.reshape(b, SSM_STATE),
                                         hi0.reshape(b, SSM_STATE), p, l, mm)
        new_states.append((s1, c1, hr1.reshape(b, G_B, P_STATE), hi1.reshape(b, G_B, P_STATE)))
        mixed = mixed.reshape(n, D_MODEL)
        if li == 0:
            xf = _out_ffn(mixed, xf, p["w_out"], p["norm_ffn"], ffn["wg"], ffn["wu"], ffn["wd"], tm, mm)
        else:
            x1, hn, comb, comb_t = _out_router(mixed, xf, p["w_out"], p["norm_ffn"], ffn["router"], tm, mm)
            xf = _moe(hn, x1, comb, comb_t, ffn["moe_wg"], ffn["moe_wu"], ffn["moe_wd"], ffn["norm_final"],
                      min(MOE_TM, n), mm)
    return xf.reshape(b, t, D_MODEL), new_states


def kernel(x_prompt, x_sample, state_delta, state_conv, state_ssm_re, state_ssm_im, meta_tokens, norm_mix, w_in, conv_w, a_log, dt_bias, norm_gate, ssm_a_re, ssm_a_im, ssm_b_re, ssm_b_im, ssm_c_re, ssm_c_im, ssm_d, ssm_log_dt, w_glu, b_glu, norm_ssm, w_out, norm_ffn, ffn_w_gate, ffn_w_up, ffn_w_down, router_w, moe_w_gate, moe_w_up, moe_w_down, norm_final):
    a = dict(norm_mix=norm_mix, w_in=w_in, conv_w=conv_w, a_log=a_log, dt_bias=dt_bias, norm_gate=norm_gate,
             ssm_a_re=ssm_a_re, ssm_a_im=ssm_a_im, ssm_b_re=ssm_b_re, ssm_b_im=ssm_b_im,
             ssm_c_re=ssm_c_re, ssm_c_im=ssm_c_im, ssm_d=ssm_d, ssm_log_dt=ssm_log_dt,
             w_glu=w_glu, b_glu=b_glu, norm_ssm=norm_ssm, w_out=w_out, norm_ffn=norm_ffn)
    depth = w_in.shape[0]

    def weights(mm):
        layers = [_layer_params(l, a, mm.act) for l in range(depth)]
        ffn = dict(wg=ffn_w_gate[0].astype(mm.act), wu=ffn_w_up[0].astype(mm.act), wd=ffn_w_down[0].astype(mm.act),
                   router=jnp.pad(router_w[0], ((0, 0), (0, LANES - N_EXPERTS))).astype(mm.act),
                   moe_wg=moe_w_gate[0].astype(mm.act), moe_wu=moe_w_up[0].astype(mm.act),
                   moe_wd=moe_w_down[0].astype(mm.act), norm_final=norm_final[None])
        return layers, ffn

    bp = x_prompt.shape[0]
    bs, ts = x_sample.shape[0], x_sample.shape[1]
    side_b = 2 * MIX_BT
    pad = side_b - bs - 1
    x_side = jnp.concatenate([x_sample, meta_tokens[None], jnp.zeros((pad, ts, D_MODEL), F32)], axis=0)

    def side_state(st):
        zeros = jnp.zeros((side_b - bs,) + st.shape[1:], F32)
        return jnp.concatenate([st, zeros], axis=0)

    side_states = [(side_state(state_delta[l]), side_state(state_conv[l]),
                    side_state(state_ssm_re[l]), side_state(state_ssm_im[l])) for l in range(depth)]
    precise = _Matmul(precise=True)
    y_side, side_new = _trunk(x_side, side_states, *weights(precise), ts, side_b * ts, precise)

    def from_meta(st):
        return jnp.broadcast_to(st[bs:bs + 1], (bp,) + st.shape[1:])

    main_states = [tuple(from_meta(st) for st in side_new[l]) for l in range(depth)]
    fast = _Matmul(precise=False)
    y_prompt, main_new = _trunk(x_prompt, main_states, *weights(fast), CHUNK, 512, fast)

    def stack(new, idx, count):
        return jnp.stack([new[l][idx][:count] for l in range(depth)])

    return (y_prompt, y_side[:bs],
            stack(main_new, 0, bp), stack(main_new, 1, bp), stack(main_new, 2, bp), stack(main_new, 3, bp),
            stack(side_new, 0, bs), stack(side_new, 1, bs), stack(side_new, 2, bs), stack(side_new, 3, bs))
```

```python
import functools

import jax
import jax.numpy as jnp
from jax import lax
from jax.experimental import pallas as pl
from jax.experimental.pallas import tpu as pltpu

F32 = jnp.float32
BF16 = jnp.bfloat16

D_MODEL = 1024
N_META = 16
H_A = 4
DK = 128
DV = 128
W_A = H_A * DV
K_CONV = 4
QKV_DIM = 2 * H_A * DK + H_A * DV
W_B = D_MODEL - W_A
GROUP_CH = 16
G_B = W_B // GROUP_CH
P_STATE = 64
SSM_STATE = G_B * P_STATE
D_FF = 2816
N_EXPERTS = 8
D_FF_EXPERT = 1408
EPS = 1e-6

LANES = 128
SUBLANES = 8
PROJ_PAD = QKV_DIM + W_A + W_B + LANES
COL_Z = QKV_DIM
COL_U = QKV_DIM + W_A
COL_BA = QKV_DIM + W_A + W_B
CONV_PAD = SUBLANES
SSM_HALF = W_B // 2
SSM_HALF_STATE = SSM_STATE // 2
MIX_BT = 8
CHUNK = 64
MOE_TM = 1024
MOE_SUB = LANES
MOE_CHUNK = 128
MOE_ALIGN = SUBLANES
VMEM_LIMIT = 56 * 1024 * 1024


class _Matmul:
    def __init__(self, precise):
        self.precise = precise
        self.act = F32 if precise else BF16

    def _args(self, a, b):
        if self.precise:
            return (a.astype(F32), b.astype(F32)), dict(precision=lax.Precision.HIGHEST)
        return (a.astype(BF16), b.astype(BF16)), {}

    def dot(self, a, b):
        args, kw = self._args(a, b)
        return jnp.dot(*args, preferred_element_type=F32, **kw)

    def einsum(self, spec, a, b):
        args, kw = self._args(a, b)
        return jnp.einsum(spec, *args, preferred_element_type=F32, **kw)


def _sigmoid(x):
    return 1.0 / (1.0 + jnp.exp(-x))


def _silu(x):
    return x * _sigmoid(x)


def _rms(x, gain):
    return x * lax.rsqrt(jnp.mean(x * x, axis=-1, keepdims=True) + EPS) * gain


def _resident(shape, grid_rank):
    zeros = (0,) * len(shape)
    index_map = (lambda i: zeros) if grid_rank == 1 else (lambda i, j: zeros)
    return pl.BlockSpec(shape, index_map, pipeline_mode=pl.Buffered(1))


def _in_proj_kernel(x_ref, g_ref, w_ref, o_ref, *, mm):
    hn = _rms(x_ref[...], g_ref[...])
    o_ref[...] = mm.dot(hn, w_ref[...])


def _in_proj(x, gain, w, tm, mm):
    n = x.shape[0]
    return pl.pallas_call(
        functools.partial(_in_proj_kernel, mm=mm),
        grid=(pl.cdiv(n, tm),),
        in_specs=[pl.BlockSpec((tm, D_MODEL), lambda i: (i, 0)),
                  _resident((1, D_MODEL), 1),
                  _resident((D_MODEL, PROJ_PAD), 1)],
        out_specs=pl.BlockSpec((tm, PROJ_PAD), lambda i: (i, 0)),
        out_shape=jax.ShapeDtypeStruct((n, PROJ_PAD), F32),
        compiler_params=pltpu.CompilerParams(dimension_semantics=("parallel",),
                                             vmem_limit_bytes=VMEM_LIMIT),
        name="in_proj",
    )(x, gain, w)


def _delta_head(q, k, v, beta, gcum, s, mm):
    bt, l, _ = q.shape
    g_hi = gcum.astype(BF16).astype(F32)
    r1 = gcum - g_hi
    g_mid = r1.astype(BF16).astype(F32)
    g_lo = r1 - g_mid
    lane = lax.broadcasted_iota(jnp.int32, (bt, l, LANES), 2)
    pieces = jnp.where(lane == 0, g_hi, jnp.where(lane == 1, g_mid, jnp.where(lane == 2, g_lo, 0.0)))
    ones = jnp.ones((bt, l, LANES), F32)
    g_row = mm.einsum("bik,bjk->bij", ones, pieces)
    ii = lax.broadcasted_iota(jnp.int32, (bt, l, l), 1)
    jj = lax.broadcasted_iota(jnp.int32, (bt, l, l), 2)
    incl = ii >= jj
    strict = ii > jj
    decay = jnp.where(incl, jnp.exp(jnp.where(incl, gcum - g_row, 0.0)), 0.0)
    kk = mm.einsum("bik,bjk->bij", k, k)
    qk = mm.einsum("bik,bjk->bij", q, k) * decay
    neg_m = jnp.where(strict, -(beta * kk * decay), 0.0)
    eye = jnp.where(ii == jj, 1.0, 0.0)
    t_inv = eye + neg_m
    p = neg_m
    size = 2
    while size < l:
        p = mm.einsum("bij,bjk->bik", p, p)
        t_inv = t_inv + mm.einsum("bij,bjk->bik", t_inv, p)
        size *= 2
    eg = jnp.exp(gcum)
    rhs = jnp.concatenate([beta * v, (beta * eg) * k], axis=-1)
    sol = mm.einsum("bij,bjd->bid", t_inv, rhs)
    u0 = sol[..., :DV]
    w = sol[..., DV:]
    u = u0 - mm.einsum("bik,bkv->biv", w, s)
    o = mm.einsum("bik,bkv->biv", q * eg, s) + mm.einsum("bij,bjv->biv", qk, u)
    g_last = gcum[:, l - 1:l, :]
    kg = k * jnp.exp(g_last - gcum)
    s_new = s * jnp.exp(g_last) + mm.einsum("bjk,bjv->bkv", kg, u)
    return o, s_new


def _mixer_kernel(qkv_ref, z_ref, u_ref, ba_ref, s0_ref, c0_ref, hr0_ref, hi0_ref,
                  convw_ref, alog_ref, dtb_ref, ngate_ref,
                  are_ref, aim_ref, bbre_ref, bbim_ref, ccre_ref, ccim_ref,
                  d_ref, wglu_ref, bglu_ref, nssm_ref,
                  out_ref, s_ref, c_ref, hr_ref, hi_ref,
                  xp_scr, utb_scr, xr_scr, xi_scr, ytb_scr, *, bt, l, mm):
    rows = bt * l
    chunk = pl.program_id(1)

    @pl.when(chunk == 0)
    def _():
        s_ref[...] = s0_ref[...]
        hr_ref[...] = hr0_ref[...]
        hi_ref[...] = hi0_ref[...]
        xp_scr[:, CONV_PAD - (K_CONV - 1):CONV_PAD, :] = c0_ref[...]

    xp_scr[:, CONV_PAD:CONV_PAD + l, :] = qkv_ref[...]
    acc = None
    for j in range(K_CONV):
        off = CONV_PAD - (K_CONV - 1) + j
        term = xp_scr[:, off:off + l, :] * convw_ref[j:j + 1, :]
        acc = term if acc is None else acc + term
    tail = xp_scr[:, CONV_PAD + l - (K_CONV - 1):CONV_PAD + l, :]
    xp_scr[:, CONV_PAD - (K_CONV - 1):CONV_PAD, :] = tail
    c_ref[...] = tail
    qkv = _silu(acc)

    ba = ba_ref[...].reshape(rows, LANES)
    beta_all = _sigmoid(ba)
    sp_in = ba + dtb_ref[...]
    softplus = jnp.maximum(sp_in, 0.0) + jnp.log1p(jnp.exp(-jnp.abs(sp_in)))
    g_all = -jnp.exp(alog_ref[...]) * softplus
    t_idx = lax.broadcasted_iota(jnp.int32, (rows, LANES), 0) % l
    shift = 1
    while shift < l:
        g_all = g_all + jnp.where(t_idx >= shift, pltpu.roll(g_all, shift, 0), 0.0)
        shift *= 2
    beta_all = beta_all.reshape(bt, l, LANES)
    g_all = g_all.reshape(bt, l, LANES)

    z = z_ref[...]
    for h in range(H_A):
        qh = qkv[:, :, h * DK:(h + 1) * DK]
        kh = qkv[:, :, (H_A + h) * DK:(H_A + h + 1) * DK]
        vh = qkv[:, :, 2 * H_A * DK + h * DV:2 * H_A * DK + (h + 1) * DV]
        qh = qh * lax.rsqrt(jnp.sum(qh * qh, axis=-1, keepdims=True) + EPS) * (DK ** -0.5)
        kh = kh * lax.rsqrt(jnp.sum(kh * kh, axis=-1, keepdims=True) + EPS)
        o, s_new = _delta_head(qh, kh, vh, beta_all[:, :, h:h + 1], g_all[:, :, H_A + h:H_A + h + 1],
                               s_ref[:, h], mm)
        s_ref[:, h] = s_new
        o = _rms(o, ngate_ref[...]) * _silu(z[:, :, h * DV:(h + 1) * DV])
        out_ref[:, :, h * DV:(h + 1) * DV] = o.astype(out_ref.dtype)

    u_bt = u_ref[...].reshape(rows, W_B)
    for c in range(W_B // LANES):
        ytb_scr[c] = u_bt[:, c * LANES:(c + 1) * LANES]
    for t in range(l):
        for c in range(W_B // LANES):
            utb_scr[t * bt:(t + 1) * bt, c * LANES:(c + 1) * LANES] = ytb_scr[c, pl.ds(t, bt, stride=l), :]
    u_tb = utb_scr[...]
    for half in range(2):
        uh = u_tb[:, half * SSM_HALF:(half + 1) * SSM_HALF]
        cols = slice(half * SSM_HALF_STATE, (half + 1) * SSM_HALF_STATE)
        xr_scr[:, cols] = mm.dot(uh, bbre_ref[half])
        xi_scr[:, cols] = mm.dot(uh, bbim_ref[half])

    for half in range(2):
        cols = slice(half * SSM_HALF_STATE, (half + 1) * SSM_HALF_STATE)
        a_re = jnp.broadcast_to(are_ref[:, cols], (bt, SSM_HALF_STATE))
        a_im = jnp.broadcast_to(aim_ref[:, cols], (bt, SSM_HALF_STATE))

        def step(t, carry):
            h_re, h_im = carry
            r = pl.multiple_of(t * bt, bt)
            n_re = a_re * h_re - a_im * h_im + xr_scr[pl.ds(r, bt), cols]
            n_im = a_re * h_im + a_im * h_re + xi_scr[pl.ds(r, bt), cols]
            xr_scr[pl.ds(r, bt), cols] = n_re
            xi_scr[pl.ds(r, bt), cols] = n_im
            return n_re, n_im

        h_re, h_im = lax.fori_loop(0, l, step, (hr_ref[:, cols], hi_ref[:, cols]))
        hr_ref[:, cols] = h_re
        hi_ref[:, cols] = h_im

    y_parts = []
    for half in range(2):
        cols = slice(half * SSM_HALF_STATE, (half + 1) * SSM_HALF_STATE)
        y_parts.append(mm.dot(xr_scr[:, cols], ccre_ref[half]) - mm.dot(xi_scr[:, cols], ccim_ref[half]))
    y = jnp.concatenate(y_parts, axis=-1) + d_ref[...] * u_tb
    gy = 0.5 * y * (1.0 + jnp.tanh(0.7978845608028654 * (y + 0.044715 * (y * y * y))))
    ob = gy * _sigmoid(mm.dot(gy, wglu_ref[...]) + bglu_ref[...])
    ob = _rms(ob, nssm_ref[...])
    for c in range(W_B // LANES):
        ytb_scr[c] = ob[:, c * LANES:(c + 1) * LANES]
    for b in range(bt):
        for c in range(W_B // LANES):
            out_ref[b, :, W_A + c * LANES:W_A + (c + 1) * LANES] = \
                ytb_scr[c, pl.ds(b, l, stride=bt), :].astype(out_ref.dtype)


def _mixer(proj, s0, c0, hr0, hi0, p, l, mm):
    b, t, _ = proj.shape
    bt = MIX_BT
    rows = bt * l
    grid = (b // bt, t // l)

    def tile(width, col_block):
        return pl.BlockSpec((bt, l, width), lambda i, n: (i, n, col_block))

    def state(shape):
        nd = len(shape)
        return pl.BlockSpec((bt,) + shape, lambda i, n: (i,) + (0,) * nd)

    def const(shape):
        return _resident(shape, 2)

    consts = [p["conv_w"], p["a_log_row"], p["dt_bias_row"], p["norm_gate"],
              p["a_re"], p["a_im"], p["bb_re"], p["bb_im"], p["cc_re"], p["cc_im"],
              p["ssm_d"], p["w_glu"], p["b_glu"], p["norm_ssm"]]
    state_shapes = [(H_A, DK, DV), (K_CONV - 1, QKV_DIM), (SSM_STATE,), (SSM_STATE,)]
    out_shapes = ([jax.ShapeDtypeStruct((b, t, D_MODEL), mm.act)]
                  + [jax.ShapeDtypeStruct((b,) + s, F32) for s in state_shapes])
    return pl.pallas_call(
        functools.partial(_mixer_kernel, bt=bt, l=l, mm=mm),
        grid=grid,
        in_specs=([tile(QKV_DIM, 0), tile(W_A, COL_Z // W_A), tile(W_B, COL_U // W_B),
                   tile(LANES, COL_BA // LANES)]
                  + [state(s) for s in state_shapes]
                  + [const(c.shape) for c in consts]),
        out_specs=[tile(D_MODEL, 0)] + [state(s) for s in state_shapes],
        out_shape=out_shapes,
        scratch_shapes=[pltpu.VMEM((bt, CONV_PAD + l, QKV_DIM), F32),
                        pltpu.VMEM((rows, W_B), F32),
                        pltpu.VMEM((rows, SSM_STATE), F32),
                        pltpu.VMEM((rows, SSM_STATE), F32),
                        pltpu.VMEM((W_B // LANES, rows, LANES), F32)],
        compiler_params=pltpu.CompilerParams(dimension_semantics=("parallel", "arbitrary"),
                                             vmem_limit_bytes=VMEM_LIMIT),
        name="mixer",
    )(proj, proj, proj, proj, s0, c0, hr0, hi0, *consts)


def _out_ffn_kernel(mix_ref, x_ref, wo_ref, g_ref, wg_ref, wu_ref, wd_ref, o_ref, *, mm):
    x1 = x_ref[...] + mm.dot(mix_ref[...], wo_ref[...])
    hn = _rms(x1, g_ref[...]).astype(mm.act)
    acc = x1
    for c in range(D_FF // D_FF_EXPERT):
        cols = slice(c * D_FF_EXPERT, (c + 1) * D_FF_EXPERT)
        hidden = _silu(mm.dot(hn, wg_ref[:, cols])) * mm.dot(hn, wu_ref[:, cols])
        acc = acc + mm.dot(hidden, wd_ref[cols, :])
    o_ref[...] = acc


def _out_ffn(mixed, x, w_out, gain, wg, wu, wd, tm, mm):
    n = x.shape[0]
    row = lambda w: pl.BlockSpec((tm, w), lambda i: (i, 0))
    const = lambda a: _resident(a.shape, 1)
    return pl.pallas_call(
        functools.partial(_out_ffn_kernel, mm=mm),
        grid=(pl.cdiv(n, tm),),
        in_specs=[row(D_MODEL), row(D_MODEL), const(w_out), const(gain), const(wg), const(wu), const(wd)],
        out_specs=row(D_MODEL),
        out_shape=jax.ShapeDtypeStruct((n, D_MODEL), F32),
        compiler_params=pltpu.CompilerParams(dimension_semantics=("parallel",),
                                             vmem_limit_bytes=VMEM_LIMIT),
        name="out_ffn",
    )(mixed, x, w_out, gain, wg, wu, wd)


def _out_router_kernel(mix_ref, x_ref, wo_ref, g_ref, wr_ref, x1_ref, hn_ref, comb_ref, slot_ref, slott_ref,
                       *, mm):
    x1 = x_ref[...] + mm.dot(mix_ref[...], wo_ref[...])
    x1_ref[...] = x1
    hn = _rms(x1, g_ref[...]).astype(mm.act)
    hn_ref[...] = hn
    logits = mm.dot(hn, wr_ref[...])
    lane = lax.broadcasted_iota(jnp.int32, logits.shape, 1).astype(F32)
    neg = jnp.float32(-jnp.inf)
    logits = jnp.where(lane < N_EXPERTS, logits, neg)
    v1 = jnp.max(logits, axis=-1, keepdims=True)
    i1 = jnp.min(jnp.where(logits == v1, lane, float(LANES)), axis=-1, keepdims=True)
    rest = jnp.where(lane == i1, neg, logits)
    v2 = jnp.max(rest, axis=-1, keepdims=True)
    i2 = jnp.min(jnp.where(rest == v2, lane, float(LANES)), axis=-1, keepdims=True)
    e2 = jnp.exp(v2 - v1)
    den = 1.0 + e2
    comb = jnp.where(lane == i1, 1.0 / den, jnp.where(lane == i2, e2 / den, 0.0))
    comb_ref[...] = comb
    routed = comb > 0.0
    routed_t = routed.astype(F32).T[:N_EXPERTS, :] > 0.0
    r_i = lax.broadcasted_iota(jnp.int32, (MOE_SUB, MOE_SUB), 0)
    c_i = lax.broadcasted_iota(jnp.int32, (MOE_SUB, MOE_SUB), 1)
    earlier_rows = jnp.where(c_i < r_i, 1.0, 0.0).astype(BF16)
    earlier_cols = jnp.where(r_i < c_i, 1.0, 0.0).astype(BF16)
    for s in range(comb.shape[0] // MOE_SUB):
        rows = slice(s * MOE_SUB, (s + 1) * MOE_SUB)
        rank = jnp.dot(earlier_rows, jnp.where(routed[rows], 1.0, 0.0).astype(BF16),
                       preferred_element_type=F32)
        slot_ref[rows, :] = jnp.where(routed[rows], rank, -1.0)
        rank_t = jnp.dot(jnp.where(routed_t[:, rows], 1.0, 0.0).astype(BF16), earlier_cols,
                         preferred_element_type=F32)
        slott_ref[:, rows] = jnp.where(routed_t[:, rows], rank_t, -1.0)


def _out_router(mixed, x, w_out, gain, w_router, tm, mm):
    n = x.shape[0]
    row = lambda w: pl.BlockSpec((tm, w), lambda i: (i, 0))
    const = lambda a: _resident(a.shape, 1)
    return pl.pallas_call(
        functools.partial(_out_router_kernel, mm=mm),
        grid=(pl.cdiv(n, tm),),
        in_specs=[row(D_MODEL), row(D_MODEL), const(w_out), const(gain), const(w_router)],
        out_specs=[row(D_MODEL), row(D_MODEL), row(LANES), row(LANES),
                   pl.BlockSpec((N_EXPERTS, tm), lambda i: (0, i))],
        out_shape=[jax.ShapeDtypeStruct((n, D_MODEL), F32),
                   jax.ShapeDtypeStruct((n, D_MODEL), mm.act),
                   jax.ShapeDtypeStruct((n, LANES), F32),
                   jax.ShapeDtypeStruct((n, LANES), F32),
                   jax.ShapeDtypeStruct((N_EXPERTS, n), F32)],
        compiler_params=pltpu.CompilerParams(dimension_semantics=("parallel",),
                                             vmem_limit_bytes=VMEM_LIMIT),
        name="out_router",
    )(mixed, x, w_out, gain, w_router)


def _moe_kernel(cnt_ref, hn_ref, x1_ref, comb_ref, slot_ref, slott_ref, wg_ref, wu_ref, wd_ref, gf_ref, o_ref,
                acc_ref, xc_ref, y_ref, *, tm, mm):
    i = pl.program_id(0)
    e = pl.program_id(1)
    nsub = tm // MOE_SUB

    @pl.when((i == 0) & (e == 0))
    def _():
        xc_ref[...] = jnp.zeros_like(xc_ref)
        y_ref[...] = jnp.zeros_like(y_ref)

    @pl.when(e == 0)
    def _():
        acc_ref[...] = jnp.zeros_like(acc_ref)

    lane = lax.broadcasted_iota(jnp.int32, (MOE_SUB, LANES), 1)
    expert_row = lax.broadcasted_iota(jnp.int32, (N_EXPERTS, MOE_SUB), 0) == e

    spans = []
    base = 0
    for s in range(nsub):
        cnt = cnt_ref[(i * nsub + s) * N_EXPERTS + e]
        spans.append(base)
        base = base + ((cnt + MOE_ALIGN - 1) // MOE_ALIGN) * MOE_ALIGN
    total = base

    slot_row = lax.broadcasted_iota(jnp.int32, (MOE_SUB, MOE_SUB), 0).astype(F32)
    for s, base_s in enumerate(spans):
        rows = slice(s * MOE_SUB, (s + 1) * MOE_SUB)
        slot = jnp.sum(jnp.where(expert_row, slott_ref[:, rows], 0.0), axis=0, keepdims=True)
        onehot = jnp.where(slot == slot_row, 1.0, 0.0).astype(BF16)
        dst = pl.ds(pl.multiple_of(base_s, MOE_ALIGN), MOE_SUB)
        xc_ref[dst, :] = mm.dot(onehot, hn_ref[rows, :])

    def ffn(c, carry):
        r = pl.ds(pl.multiple_of(c * MOE_CHUNK, MOE_CHUNK), MOE_CHUNK)
        x = xc_ref[r, :].astype(mm.act)
        hidden = _silu(mm.dot(x, wg_ref[0])) * mm.dot(x, wu_ref[0])
        y_ref[r, :] = mm.dot(hidden, wd_ref[0])
        return carry

    lax.fori_loop(0, (total + MOE_CHUNK - 1) // MOE_CHUNK, ffn, 0)

    pieces = 1 if mm.precise else 2
    slot_lane = lax.broadcasted_iota(jnp.int32, (MOE_SUB, pieces * MOE_SUB), 1)
    slot_lane = jnp.where(slot_lane >= MOE_SUB, slot_lane - MOE_SUB, slot_lane).astype(F32)
    for s, base_s in enumerate(spans):
        rows = slice(s * MOE_SUB, (s + 1) * MOE_SUB)
        gate = jnp.sum(jnp.where(lane == e, comb_ref[rows, :], 0.0), axis=-1, keepdims=True)
        slot = jnp.sum(jnp.where(lane == e, slot_ref[rows, :], 0.0), axis=-1, keepdims=True)
        y = y_ref[pl.ds(pl.multiple_of(base_s, MOE_ALIGN), MOE_SUB), :]
        if not mm.precise:
            y_hi = y.astype(BF16)
            y = jnp.concatenate([y_hi, (y - y_hi.astype(F32)).astype(BF16)], axis=0)
        onehot = jnp.where(slot == slot_lane, 1.0, 0.0)
        back = mm.dot(onehot, y)
        acc_ref[rows, :] += gate * back

    @pl.when(e == N_EXPERTS - 1)
    def _():
        o_ref[...] = _rms(x1_ref[...] + acc_ref[...], gf_ref[...])


def _moe(hn, x1, comb, slot, slot_t, wg, wu, wd, gain_final, tm, mm):
    n = x1.shape[0]
    assert n % tm == 0 and tm % MOE_SUB == 0
    routed = comb[:, :N_EXPERTS] > 0.0
    counts = jnp.sum(routed.reshape(n // MOE_SUB, MOE_SUB, N_EXPERTS), axis=1, dtype=jnp.int32).reshape(-1)
    row = lambda w: pl.BlockSpec((tm, w), lambda i, e, cnt: (i, 0))
    expert = lambda a, b: pl.BlockSpec((1, a, b), lambda i, e, cnt: (e, 0, 0))
    packed_rows = tm + max(MOE_CHUNK, MOE_SUB) + (tm // MOE_SUB) * MOE_ALIGN
    return pl.pallas_call(
        functools.partial(_moe_kernel, tm=tm, mm=mm),
        grid_spec=pltpu.PrefetchScalarGridSpec(
            num_scalar_prefetch=1,
            grid=(n // tm, N_EXPERTS),
            in_specs=[row(D_MODEL), row(D_MODEL), row(LANES), row(LANES),
                      pl.BlockSpec((N_EXPERTS, tm), lambda i, e, cnt: (0, i)),
                      expert(D_MODEL, D_FF_EXPERT), expert(D_MODEL, D_FF_EXPERT), expert(D_FF_EXPERT, D_MODEL),
                      pl.BlockSpec((1, D_MODEL), lambda i, e, cnt: (0, 0))],
            out_specs=row(D_MODEL),
            scratch_shapes=[pltpu.VMEM((tm, D_MODEL), F32),
                            pltpu.VMEM((packed_rows, D_MODEL), F32),
                            pltpu.VMEM((packed_rows, D_MODEL), F32)]),
        out_shape=jax.ShapeDtypeStruct((n, D_MODEL), F32),
        compiler_params=pltpu.CompilerParams(dimension_semantics=("arbitrary", "arbitrary"),
                                             vmem_limit_bytes=VMEM_LIMIT),
        name="moe",
    )(counts, hn, x1, comb, slot, slot_t, wg, wu, wd, gain_final)


def _lane_row(vec, offset):
    return jnp.zeros((1, LANES), F32).at[0, offset:offset + vec.shape[0]].set(vec.astype(F32))


def _block_diag_halves(w):
    g, a, b = w.shape
    hg = g // 2
    w = w.reshape(2, hg, a, b)
    eye = jnp.eye(hg, dtype=w.dtype)
    return jnp.einsum("sgab,gh->sgahb", w, eye).reshape(2, hg * a, hg * b)


MATMUL_WEIGHTS = ("w_in", "bb_re", "bb_im", "cc_re", "cc_im", "w_glu", "w_out")


def _layer_params(l, a):
    w_in = a["w_in"][l]
    w_cat = jnp.concatenate([w_in[:, :QKV_DIM + W_A], w_in[:, QKV_DIM + W_A + 2 * H_A:],
                             w_in[:, QKV_DIM + W_A:QKV_DIM + W_A + 2 * H_A],
                             jnp.zeros((D_MODEL, LANES - 2 * H_A), F32)], axis=1)
    lam_re = a["ssm_a_re"][l]
    lam_im = a["ssm_a_im"][l]
    delta = jnp.exp(a["ssm_log_dt"][l])[:, None]
    mag = jnp.exp(lam_re * delta)
    ab_re = mag * jnp.cos(lam_im * delta)
    ab_im = mag * jnp.sin(lam_im * delta)
    den = lam_re * lam_re + lam_im * lam_im
    f_re = ((ab_re - 1.0) * lam_re + ab_im * lam_im) / den
    f_im = (ab_im * lam_re - (ab_re - 1.0) * lam_im) / den
    b_re = a["ssm_b_re"][l]
    b_im = a["ssm_b_im"][l]
    bb_re = f_re[..., None] * b_re - f_im[..., None] * b_im
    bb_im = f_re[..., None] * b_im + f_im[..., None] * b_re
    return dict(
        norm_mix=a["norm_mix"][l][None], w_in=w_cat,
        conv_w=a["conv_w"][l],
        a_log_row=_lane_row(a["a_log"][l], H_A), dt_bias_row=_lane_row(a["dt_bias"][l], H_A),
        norm_gate=a["norm_gate"][l][None],
        a_re=ab_re.reshape(1, SSM_STATE), a_im=ab_im.reshape(1, SSM_STATE),
        bb_re=_block_diag_halves(jnp.swapaxes(bb_re, 1, 2)),
        bb_im=_block_diag_halves(jnp.swapaxes(bb_im, 1, 2)),
        cc_re=_block_diag_halves(jnp.swapaxes(a["ssm_c_re"][l], 1, 2)),
        cc_im=_block_diag_halves(jnp.swapaxes(a["ssm_c_im"][l], 1, 2)),
        ssm_d=a["ssm_d"][l][None], w_glu=a["w_glu"][l], b_glu=a["b_glu"][l][None],
        norm_ssm=a["norm_ssm"][l][None],
        w_out=a["w_out"][l], norm_ffn=a["norm_ffn"][l][None])


def _trunk(x, states, layers, ffn, moe, l, tm, mm):
    b, t, _ = x.shape
    n = b * t
    xf = x.reshape(n, D_MODEL)
    new_states = []
    for li, p in enumerate(layers):
        proj = _in_proj(xf, p["norm_mix"], p["w_in"], tm, mm).reshape(b, t, PROJ_PAD)
        s0, c0, hr0, hi0 = states[li]
        mixed, s1, c1, hr1, hi1 = _mixer(proj, s0, c0, hr0.reshape(b, SSM_STATE),
                                         hi0.reshape(b, SSM_STATE), p, l, mm)
        new_states.append((s1, c1, hr1.reshape(b, G_B, P_STATE), hi1.reshape(b, G_B, P_STATE)))
        mixed = mixed.reshape(n, D_MODEL)
        if li == 0:
            xf = _out_ffn(mixed, xf, p["w_out"], p["norm_ffn"], ffn["wg"], ffn["wu"], ffn["wd"], tm, mm)
        else:
            x1, hn, comb, slot, slot_t = _out_router(mixed, xf, p["w_out"], p["norm_ffn"], ffn["router"], tm, mm)
            xf = _moe(hn, x1, comb, slot, slot_t, moe["wg"], moe["wu"], moe["wd"], moe["norm_final"],
                      min(MOE_TM, n), _Matmul(precise=False))
    return xf.reshape(b, t, D_MODEL), new_states


def kernel(x_prompt, x_sample, state_delta, state_conv, state_ssm_re, state_ssm_im, meta_tokens, norm_mix, w_in, conv_w, a_log, dt_bias, norm_gate, ssm_a_re, ssm_a_im, ssm_b_re, ssm_b_im, ssm_c_re, ssm_c_im, ssm_d, ssm_log_dt, w_glu, b_glu, norm_ssm, w_out, norm_ffn, ffn_w_gate, ffn_w_up, ffn_w_down, router_w, moe_w_gate, moe_w_up, moe_w_down, norm_final):
    a = dict(norm_mix=norm_mix, w_in=w_in, conv_w=conv_w, a_log=a_log, dt_bias=dt_bias, norm_gate=norm_gate,
             ssm_a_re=ssm_a_re, ssm_a_im=ssm_a_im, ssm_b_re=ssm_b_re, ssm_b_im=ssm_b_im,
             ssm_c_re=ssm_c_re, ssm_c_im=ssm_c_im, ssm_d=ssm_d, ssm_log_dt=ssm_log_dt,
             w_glu=w_glu, b_glu=b_glu, norm_ssm=norm_ssm, w_out=w_out, norm_ffn=norm_ffn)
    depth = w_in.shape[0]

    layers32 = [_layer_params(l, a) for l in range(depth)]
    ffn32 = dict(wg=ffn_w_gate[0], wu=ffn_w_up[0], wd=ffn_w_down[0],
                 router=jnp.pad(router_w[0], ((0, 0), (0, LANES - N_EXPERTS))))
    moe = dict(wg=moe_w_gate[0].astype(BF16), wu=moe_w_up[0].astype(BF16), wd=moe_w_down[0].astype(BF16),
               norm_final=norm_final[None])

    def weights(mm):
        layers = [{k: (v.astype(mm.act) if k in MATMUL_WEIGHTS else v) for k, v in p.items()} for p in layers32]
        return layers, {k: v.astype(mm.act) for k, v in ffn32.items()}, moe

    bp = x_prompt.shape[0]
    bs, ts = x_sample.shape[0], x_sample.shape[1]
    side_b = 2 * MIX_BT
    pad = side_b - bs - 1
    x_side = jnp.concatenate([x_sample, meta_tokens[None], jnp.zeros((pad, ts, D_MODEL), F32)], axis=0)

    def side_state(st):
        zeros = jnp.zeros((side_b - bs,) + st.shape[1:], F32)
        return jnp.concatenate([st, zeros], axis=0)

    side_states = [(side_state(state_delta[l]), side_state(state_conv[l]),
                    side_state(state_ssm_re[l]), side_state(state_ssm_im[l])) for l in range(depth)]
    precise = _Matmul(precise=True)
    y_side, side_new = _trunk(x_side, side_states, *weights(precise), ts, side_b * ts, precise)

    def from_meta(st):
        return jnp.broadcast_to(st[bs:bs + 1], (bp,) + st.shape[1:])

    main_states = [tuple(from_meta(st) for st in side_new[l]) for l in range(depth)]
    fast = _Matmul(precise=False)
    y_prompt, main_new = _trunk(x_prompt, main_states, *weights(fast), CHUNK, 512, fast)

    def stack(new, idx, count):
        return jnp.stack([new[l][idx][:count] for l in range(depth)])

    return (y_prompt, y_side[:bs],
            stack(main_new, 0, bp), stack(main_new, 1, bp), stack(main_new, 2, bp), stack(main_new, 3, bp),
            stack(side_new, 0, bs), stack(side_new, 1, bs), stack(side_new, 2, bs), stack(side_new, 3, bs))
```

```python
import functools

import jax
import jax.numpy as jnp
from jax import lax
from jax.experimental import pallas as pl
from jax.experimental.pallas import tpu as pltpu

F32 = jnp.float32
BF16 = jnp.bfloat16

D_MODEL = 1024
N_META = 16
H_A = 4
DK = 128
DV = 128
W_A = H_A * DV
K_CONV = 4
QKV_DIM = 2 * H_A * DK + H_A * DV
W_B = D_MODEL - W_A
GROUP_CH = 16
G_B = W_B // GROUP_CH
P_STATE = 64
SSM_STATE = G_B * P_STATE
D_FF = 2816
N_EXPERTS = 8
TOP_K = 2
D_FF_EXPERT = 1408
EPS = 1e-6

LANES = 128
SUBLANES = 8
PROJ_PAD = QKV_DIM + W_A + W_B + LANES
COL_Z = QKV_DIM
COL_U = QKV_DIM + W_A
COL_BA = QKV_DIM + W_A + W_B
CONV_PAD = SUBLANES
SSM_HALF = W_B // 2
SSM_HALF_STATE = SSM_STATE // 2
MIX_BT = 8
CHUNK = 64
MOE_TM = 1024
MOE_SUB = LANES
MOE_STATIC = MOE_TM * TOP_K // N_EXPERTS
MOE_CHUNK = 64
MOE_ALIGN = SUBLANES
VMEM_LIMIT = 60 * 1024 * 1024


class _Matmul:
    def __init__(self, precise):
        self.precise = precise
        self.act = F32 if precise else BF16

    def _args(self, a, b):
        if self.precise:
            return (a.astype(F32), b.astype(F32)), dict(precision=lax.Precision.HIGHEST)
        return (a.astype(BF16), b.astype(BF16)), {}

    def dot(self, a, b):
        args, kw = self._args(a, b)
        return jnp.dot(*args, preferred_element_type=F32, **kw)

    def einsum(self, spec, a, b):
        args, kw = self._args(a, b)
        return jnp.einsum(spec, *args, preferred_element_type=F32, **kw)


def _sigmoid(x):
    return 1.0 / (1.0 + jnp.exp(-x))


def _silu(x):
    return x * _sigmoid(x)


def _rms(x, gain):
    return x * lax.rsqrt(jnp.mean(x * x, axis=-1, keepdims=True) + EPS) * gain


def _resident(shape, grid_rank):
    zeros = (0,) * len(shape)
    index_map = (lambda i: zeros) if grid_rank == 1 else (lambda i, j: zeros)
    return pl.BlockSpec(shape, index_map, pipeline_mode=pl.Buffered(1))


def _delta_head(q, k, v, beta, gcum, s, mm):
    bt, l, _ = q.shape
    g_hi = gcum.astype(BF16).astype(F32)
    r1 = gcum - g_hi
    g_mid = r1.astype(BF16).astype(F32)
    g_lo = r1 - g_mid
    lane = lax.broadcasted_iota(jnp.int32, (bt, l, LANES), 2)
    pieces = jnp.where(lane == 0, g_hi, jnp.where(lane == 1, g_mid, jnp.where(lane == 2, g_lo, 0.0)))
    ones = jnp.ones((bt, l, LANES), F32)
    g_row = mm.einsum("bik,bjk->bij", ones, pieces)
    ii = lax.broadcasted_iota(jnp.int32, (bt, l, l), 1)
    jj = lax.broadcasted_iota(jnp.int32, (bt, l, l), 2)
    incl = ii >= jj
    strict = ii > jj
    decay = jnp.where(incl, jnp.exp(jnp.where(incl, gcum - g_row, 0.0)), 0.0)
    kk = mm.einsum("bik,bjk->bij", k, k)
    qk = mm.einsum("bik,bjk->bij", q, k) * decay
    neg_m = jnp.where(strict, -(beta * kk * decay), 0.0)
    eye = jnp.where(ii == jj, 1.0, 0.0)
    t_inv = eye + neg_m
    p = neg_m
    size = 2
    while size < l:
        p = mm.einsum("bij,bjk->bik", p, p)
        t_inv = t_inv + mm.einsum("bij,bjk->bik", t_inv, p)
        size *= 2
    eg = jnp.exp(gcum)
    rhs = jnp.concatenate([beta * v, (beta * eg) * k], axis=-1)
    sol = mm.einsum("bij,bjd->bid", t_inv, rhs)
    u0 = sol[..., :DV]
    w = sol[..., DV:]
    u = u0 - mm.einsum("bik,bkv->biv", w, s)
    o = mm.einsum("bik,bkv->biv", q * eg, s) + mm.einsum("bij,bjv->biv", qk, u)
    g_last = gcum[:, l - 1:l, :]
    kg = k * jnp.exp(g_last - gcum)
    s_new = s * jnp.exp(g_last) + mm.einsum("bjk,bjv->bkv", kg, u)
    return o, s_new


def _mixer_kernel(xfirst_ref, xnext_ref, gin_ref, win_ref, s0_ref, c0_ref, hr0_ref, hi0_ref,
                  convw_ref, alog_ref, dtb_ref, ngate_ref,
                  are_ref, aim_ref, bbre_ref, bbim_ref, ccre_ref, ccim_ref,
                  d_ref, wglu_ref, bglu_ref, nssm_ref,
                  out_ref, s_ref, c_ref, hr_ref, hi_ref,
                  proj_scr, xp_scr, utb_scr, xr_scr, xi_scr, ytb_scr, *, bt, l, mm):
    rows = bt * l
    chunk = pl.program_id(1)

    def project(x_ref):
        hn = _rms(x_ref[...].reshape(rows, D_MODEL), gin_ref[...])
        return mm.dot(hn, win_ref[...])

    @pl.when(chunk == 0)
    def _():
        s_ref[...] = s0_ref[...]
        hr_ref[...] = hr0_ref[...]
        hi_ref[...] = hi0_ref[...]
        xp_scr[:, CONV_PAD - (K_CONV - 1):CONV_PAD, :] = c0_ref[...]
        proj_scr[0:rows, :] = project(xfirst_ref)

    slot = chunk % 2
    cur = pl.ds(pl.multiple_of(slot * rows, rows), rows)

    def proj_cols(start, width):
        return proj_scr[cur, start:start + width].reshape(bt, l, width)

    xp_scr[:, CONV_PAD:CONV_PAD + l, :] = proj_cols(0, QKV_DIM)
    acc = None
    for j in range(K_CONV):
        off = CONV_PAD - (K_CONV - 1) + j
        term = xp_scr[:, off:off + l, :] * convw_ref[j:j + 1, :]
        acc = term if acc is None else acc + term
    tail = xp_scr[:, CONV_PAD + l - (K_CONV - 1):CONV_PAD + l, :]
    xp_scr[:, CONV_PAD - (K_CONV - 1):CONV_PAD, :] = tail
    c_ref[...] = tail
    qkv = _silu(acc)

    ba = proj_scr[cur, COL_BA:COL_BA + LANES]
    beta_all = _sigmoid(ba)
    sp_in = ba + dtb_ref[...]
    softplus = jnp.maximum(sp_in, 0.0) + jnp.log1p(jnp.exp(-jnp.abs(sp_in)))
    g_all = -jnp.exp(alog_ref[...]) * softplus
    t_idx = lax.broadcasted_iota(jnp.int32, (rows, LANES), 0) % l
    shift = 1
    while shift < l:
        g_all = g_all + jnp.where(t_idx >= shift, pltpu.roll(g_all, shift, 0), 0.0)
        shift *= 2
    beta_all = beta_all.reshape(bt, l, LANES)
    g_all = g_all.reshape(bt, l, LANES)

    def heads(first_col, width, src):
        return jnp.concatenate([src[:, :, first_col + h * width:first_col + (h + 1) * width]
                                for h in range(H_A)], axis=0)

    z = proj_cols(COL_Z, W_A)
    q = heads(0, DK, qkv)
    k = heads(H_A * DK, DK, qkv)
    v = heads(2 * H_A * DK, DV, qkv)
    q = q * lax.rsqrt(jnp.sum(q * q, axis=-1, keepdims=True) + EPS) * (DK ** -0.5)
    k = k * lax.rsqrt(jnp.sum(k * k, axis=-1, keepdims=True) + EPS)
    s_all = jnp.concatenate([s_ref[:, h] for h in range(H_A)], axis=0)
    o, s_new = _delta_head(q, k, v, heads(0, 1, beta_all), heads(H_A, 1, g_all), s_all, mm)
    o = _rms(o, ngate_ref[...]) * _silu(heads(0, DV, z))
    for h in range(H_A):
        s_ref[:, h] = s_new[h * bt:(h + 1) * bt]
        out_ref[:, :, h * DV:(h + 1) * DV] = o[h * bt:(h + 1) * bt].astype(out_ref.dtype)

    u_bt = proj_scr[cur, COL_U:COL_U + W_B]
    for c in range(W_B // LANES):
        ytb_scr[c] = u_bt[:, c * LANES:(c + 1) * LANES]
    for t in range(l):
        for c in range(W_B // LANES):
            utb_scr[t * bt:(t + 1) * bt, c * LANES:(c + 1) * LANES] = ytb_scr[c, pl.ds(t, bt, stride=l), :]
    u_tb = utb_scr[...]
    for half in range(2):
        uh = u_tb[:, half * SSM_HALF:(half + 1) * SSM_HALF]
        cols = slice(half * SSM_HALF_STATE, (half + 1) * SSM_HALF_STATE)
        xr_scr[:, cols] = mm.dot(uh, bbre_ref[half])
        xi_scr[:, cols] = mm.dot(uh, bbim_ref[half])

    for half in range(2):
        cols = slice(half * SSM_HALF_STATE, (half + 1) * SSM_HALF_STATE)
        a_re = jnp.broadcast_to(are_ref[:, cols], (bt, SSM_HALF_STATE))
        a_im = jnp.broadcast_to(aim_ref[:, cols], (bt, SSM_HALF_STATE))

        def step(t, carry):
            h_re, h_im = carry
            r = pl.multiple_of(t * bt, bt)
            n_re = a_re * h_re - a_im * h_im + xr_scr[pl.ds(r, bt), cols]
            n_im = a_re * h_im + a_im * h_re + xi_scr[pl.ds(r, bt), cols]
            xr_scr[pl.ds(r, bt), cols] = n_re
            xi_scr[pl.ds(r, bt), cols] = n_im
            return n_re, n_im

        h_re, h_im = lax.fori_loop(0, l, step, (hr_ref[:, cols], hi_ref[:, cols]))
        hr_ref[:, cols] = h_re
        hi_ref[:, cols] = h_im

    y_parts = []
    for half in range(2):
        cols = slice(half * SSM_HALF_STATE, (half + 1) * SSM_HALF_STATE)
        y_parts.append(mm.dot(xr_scr[:, cols], ccre_ref[half]) - mm.dot(xi_scr[:, cols], ccim_ref[half]))
    y = jnp.concatenate(y_parts, axis=-1) + d_ref[...] * u_tb
    gy = 0.5 * y * (1.0 + jnp.tanh(0.7978845608028654 * (y + 0.044715 * (y * y * y))))
    ob = gy * _sigmoid(mm.dot(gy, wglu_ref[...]) + bglu_ref[...])
    ob = _rms(ob, nssm_ref[...])
    for c in range(W_B // LANES):
        ytb_scr[c] = ob[:, c * LANES:(c + 1) * LANES]
    for b in range(bt):
        for c in range(W_B // LANES):
            out_ref[b, :, W_A + c * LANES:W_A + (c + 1) * LANES] = \
                ytb_scr[c, pl.ds(b, l, stride=bt), :].astype(out_ref.dtype)

    proj_scr[pl.ds(pl.multiple_of((1 - slot) * rows, rows), rows), :] = project(xnext_ref)


def _mixer(x, s0, c0, hr0, hi0, p, l, mm):
    b, t, _ = x.shape
    bt = MIX_BT
    rows = bt * l
    nchunks = t // l
    grid = (b // bt, nchunks)

    def state(shape, **kw):
        nd = len(shape)
        return pl.BlockSpec((bt,) + shape, lambda i, n: (i,) + (0,) * nd, **kw)

    def const(shape):
        return _resident(shape, 2)

    x_first = pl.BlockSpec((bt, l, D_MODEL), lambda i, n: (i, 0, 0), pipeline_mode=pl.Buffered(1))
    x_next = pl.BlockSpec((bt, l, D_MODEL), lambda i, n: (i, jnp.minimum(n + 1, nchunks - 1), 0))
    consts = [p["norm_mix"], p["w_in"]]
    consts_tail = [p["conv_w"], p["a_log_row"], p["dt_bias_row"], p["norm_gate"],
              p["a_re"], p["a_im"], p["bb_re"], p["bb_im"], p["cc_re"], p["cc_im"],
              p["ssm_d"], p["w_glu"], p["b_glu"], p["norm_ssm"]]
    state_shapes = [(H_A, DK, DV), (K_CONV - 1, QKV_DIM), (SSM_STATE,), (SSM_STATE,)]
    out_shapes = ([jax.ShapeDtypeStruct((b, t, D_MODEL), mm.act)]
                  + [jax.ShapeDtypeStruct((b,) + s, F32) for s in state_shapes])
    return pl.pallas_call(
        functools.partial(_mixer_kernel, bt=bt, l=l, mm=mm),
        grid=grid,
        in_specs=([x_first, x_next] + [const(c.shape) for c in consts]
                  + [state(s, pipeline_mode=pl.Buffered(1)) for s in state_shapes]
                  + [const(c.shape) for c in consts_tail]),
        out_specs=[pl.BlockSpec((bt, l, D_MODEL), lambda i, n: (i, n, 0))] + [state(s) for s in state_shapes],
        out_shape=out_shapes,
        scratch_shapes=[pltpu.VMEM((2 * rows, PROJ_PAD), F32),
                        pltpu.VMEM((bt, CONV_PAD + l, QKV_DIM), F32),
                        pltpu.VMEM((rows, W_B), F32),
                        pltpu.VMEM((rows, SSM_STATE), F32),
                        pltpu.VMEM((rows, SSM_STATE), F32),
                        pltpu.VMEM((W_B // LANES, rows, LANES), F32)],
        compiler_params=pltpu.CompilerParams(dimension_semantics=("parallel", "arbitrary"),
                                             vmem_limit_bytes=VMEM_LIMIT),
        name="mixer",
    )(x, x, *consts, s0, c0, hr0, hi0, *consts_tail)


def _out_ffn_kernel(mix_ref, x_ref, wo_ref, g_ref, wg_ref, wu_ref, wd_ref, o_ref, *, mm):
    x1 = x_ref[...] + mm.dot(mix_ref[...], wo_ref[...])
    hn = _rms(x1, g_ref[...]).astype(mm.act)
    acc = x1
    for c in range(D_FF // D_FF_EXPERT):
        cols = slice(c * D_FF_EXPERT, (c + 1) * D_FF_EXPERT)
        hidden = _silu(mm.dot(hn, wg_ref[:, cols])) * mm.dot(hn, wu_ref[:, cols])
        acc = acc + mm.dot(hidden, wd_ref[cols, :])
    o_ref[...] = acc


def _out_ffn(mixed, x, w_out, gain, wg, wu, wd, tm, mm):
    n = x.shape[0]
    row = lambda w: pl.BlockSpec((tm, w), lambda i: (i, 0))
    const = lambda a: _resident(a.shape, 1)
    return pl.pallas_call(
        functools.partial(_out_ffn_kernel, mm=mm),
        grid=(pl.cdiv(n, tm),),
        in_specs=[row(D_MODEL), row(D_MODEL), const(w_out), const(gain), const(wg), const(wu), const(wd)],
        out_specs=row(D_MODEL),
        out_shape=jax.ShapeDtypeStruct((n, D_MODEL), F32),
        compiler_params=pltpu.CompilerParams(dimension_semantics=("parallel",),
                                             vmem_limit_bytes=VMEM_LIMIT),
        name="out_ffn",
    )(mixed, x, w_out, gain, wg, wu, wd)


def _out_router_kernel(mix_ref, x_ref, wo_ref, g_ref, wr_ref, x1_ref, hn_ref, comb_ref, slot_ref, slott_ref,
                       *, mm):
    x1 = x_ref[...] + mm.dot(mix_ref[...], wo_ref[...])
    x1_ref[...] = x1
    hn = _rms(x1, g_ref[...]).astype(mm.act)
    hn_ref[...] = hn
    logits = mm.dot(hn, wr_ref[...])
    lane = lax.broadcasted_iota(jnp.int32, logits.shape, 1).astype(F32)
    neg = jnp.float32(-jnp.inf)
    logits = jnp.where(lane < N_EXPERTS, logits, neg)
    v1 = jnp.max(logits, axis=-1, keepdims=True)
    i1 = jnp.min(jnp.where(logits == v1, lane, float(LANES)), axis=-1, keepdims=True)
    rest = jnp.where(lane == i1, neg, logits)
    v2 = jnp.max(rest, axis=-1, keepdims=True)
    i2 = jnp.min(jnp.where(rest == v2, lane, float(LANES)), axis=-1, keepdims=True)
    e2 = jnp.exp(v2 - v1)
    den = 1.0 + e2
    comb = jnp.where(lane == i1, 1.0 / den, jnp.where(lane == i2, e2 / den, 0.0))
    comb_ref[...] = comb
    routed = comb > 0.0
    routed_t = routed.astype(F32).T[:N_EXPERTS, :] > 0.0
    r_i = lax.broadcasted_iota(jnp.int32, (MOE_SUB, MOE_SUB), 0)
    c_i = lax.broadcasted_iota(jnp.int32, (MOE_SUB, MOE_SUB), 1)
    earlier_rows = jnp.where(c_i < r_i, 1.0, 0.0).astype(BF16)
    earlier_cols = jnp.where(r_i < c_i, 1.0, 0.0).astype(BF16)
    for s in range(comb.shape[0] // MOE_SUB):
        rows = slice(s * MOE_SUB, (s + 1) * MOE_SUB)
        rank = jnp.dot(earlier_rows, jnp.where(routed[rows], 1.0, 0.0).astype(BF16),
                       preferred_element_type=F32)
        slot_ref[rows, :] = jnp.where(routed[rows], rank, -1.0)
        rank_t = jnp.dot(jnp.where(routed_t[:, rows], 1.0, 0.0).astype(BF16), earlier_cols,
                         preferred_element_type=F32)
        slott_ref[:, rows] = jnp.where(routed_t[:, rows], rank_t, -1.0)


def _out_router(mixed, x, w_out, gain, w_router, tm, mm):
    n = x.shape[0]
    row = lambda w: pl.BlockSpec((tm, w), lambda i: (i, 0))
    const = lambda a: _resident(a.shape, 1)
    return pl.pallas_call(
        functools.partial(_out_router_kernel, mm=mm),
        grid=(pl.cdiv(n, tm),),
        in_specs=[row(D_MODEL), row(D_MODEL), const(w_out), const(gain), const(w_router)],
        out_specs=[row(D_MODEL), row(D_MODEL), row(LANES), row(LANES),
                   pl.BlockSpec((N_EXPERTS, tm), lambda i: (0, i))],
        out_shape=[jax.ShapeDtypeStruct((n, D_MODEL), F32),
                   jax.ShapeDtypeStruct((n, D_MODEL), mm.act),
                   jax.ShapeDtypeStruct((n, LANES), F32),
                   jax.ShapeDtypeStruct((n, LANES), F32),
                   jax.ShapeDtypeStruct((N_EXPERTS, n), F32)],
        compiler_params=pltpu.CompilerParams(dimension_semantics=("parallel",),
                                             vmem_limit_bytes=VMEM_LIMIT),
        name="out_router",
    )(mixed, x, w_out, gain, w_router)


def _moe_kernel(cnt_ref, hn_ref, x1_ref, comb_ref, slot_ref, slott_ref, wg_ref, wu_ref, wd_ref, gf_ref, o_ref,
                acc_ref, xc_ref, y_ref, *, tm, mm):
    i = pl.program_id(0)
    e = pl.program_id(1)
    nsub = tm // MOE_SUB

    @pl.when((i == 0) & (e == 0))
    def _():
        xc_ref[...] = jnp.zeros_like(xc_ref)
        y_ref[...] = jnp.zeros_like(y_ref)

    @pl.when(e == 0)
    def _():
        acc_ref[...] = jnp.zeros_like(acc_ref)

    lane = lax.broadcasted_iota(jnp.int32, (MOE_SUB, LANES), 1)
    expert_row = lax.broadcasted_iota(jnp.int32, (N_EXPERTS, MOE_SUB), 0) == e

    spans = []
    base = 0
    for s in range(nsub):
        cnt = cnt_ref[(i * nsub + s) * N_EXPERTS + e]
        spans.append(base)
        base = base + ((cnt + MOE_ALIGN - 1) // MOE_ALIGN) * MOE_ALIGN
    total = base

    slot_row = lax.broadcasted_iota(jnp.int32, (MOE_SUB, MOE_SUB), 0).astype(F32)
    for s, base_s in enumerate(spans):
        rows = slice(s * MOE_SUB, (s + 1) * MOE_SUB)
        slot = jnp.sum(jnp.where(expert_row, slott_ref[:, rows], 0.0), axis=0, keepdims=True)
        onehot = jnp.where(slot == slot_row, 1.0, 0.0).astype(BF16)
        dst = pl.ds(pl.multiple_of(base_s, MOE_ALIGN), MOE_SUB)
        xc_ref[dst, :] = mm.dot(onehot, hn_ref[rows, :])

    def ffn(r):
        x = xc_ref[r, :].astype(mm.act)
        hidden = _silu(mm.dot(x, wg_ref[0])) * mm.dot(x, wu_ref[0])
        y_ref[r, :] = mm.dot(hidden, wd_ref[0])

    static_rows = min(MOE_STATIC, tm)
    ffn(slice(0, static_rows))

    def ffn_chunk(c, carry):
        ffn(pl.ds(pl.multiple_of(static_rows + c * MOE_CHUNK, MOE_CHUNK), MOE_CHUNK))
        return carry

    lax.fori_loop(0, (jnp.maximum(total - static_rows, 0) + MOE_CHUNK - 1) // MOE_CHUNK, ffn_chunk, 0)

    pieces = 1 if mm.precise else 2
    slot_lane = lax.broadcasted_iota(jnp.int32, (MOE_SUB, pieces * MOE_SUB), 1)
    slot_lane = jnp.where(slot_lane >= MOE_SUB, slot_lane - MOE_SUB, slot_lane).astype(F32)
    for s, base_s in enumerate(spans):
        rows = slice(s * MOE_SUB, (s + 1) * MOE_SUB)
        gate = jnp.sum(jnp.where(lane == e, comb_ref[rows, :], 0.0), axis=-1, keepdims=True)
        slot = jnp.sum(jnp.where(lane == e, slot_ref[rows, :], 0.0), axis=-1, keepdims=True)
        y = y_ref[pl.ds(pl.multiple_of(base_s, MOE_ALIGN), MOE_SUB), :]
        if not mm.precise:
            y_hi = y.astype(BF16)
            y = jnp.concatenate([y_hi, (y - y_hi.astype(F32)).astype(BF16)], axis=0)
        onehot = jnp.where(slot == slot_lane, 1.0, 0.0)
        back = mm.dot(onehot, y)
        acc_ref[rows, :] += gate * back

    @pl.when(e == N_EXPERTS - 1)
    def _():
        o_ref[...] = _rms(x1_ref[...] + acc_ref[...], gf_ref[...])


def _moe(hn, x1, comb, slot, slot_t, wg, wu, wd, gain_final, tm, mm):
    n = x1.shape[0]
    assert n % tm == 0 and tm % MOE_SUB == 0
    routed = comb[:, :N_EXPERTS] > 0.0
    counts = jnp.sum(routed.reshape(n // MOE_SUB, MOE_SUB, N_EXPERTS), axis=1, dtype=jnp.int32).reshape(-1)
    row = lambda w: pl.BlockSpec((tm, w), lambda i, e, cnt: (i, 0))
    expert = lambda a, b: pl.BlockSpec((1, a, b), lambda i, e, cnt: (e, 0, 0))
    packed_rows = tm + max(MOE_CHUNK, MOE_SUB) + (tm // MOE_SUB) * MOE_ALIGN
    return pl.pallas_call(
        functools.partial(_moe_kernel, tm=tm, mm=mm),
        grid_spec=pltpu.PrefetchScalarGridSpec(
            num_scalar_prefetch=1,
            grid=(n // tm, N_EXPERTS),
            in_specs=[row(D_MODEL), row(D_MODEL), row(LANES), row(LANES),
                      pl.BlockSpec((N_EXPERTS, tm), lambda i, e, cnt: (0, i)),
                      expert(D_MODEL, D_FF_EXPERT), expert(D_MODEL, D_FF_EXPERT), expert(D_FF_EXPERT, D_MODEL),
                      pl.BlockSpec((1, D_MODEL), lambda i, e, cnt: (0, 0))],
            out_specs=row(D_MODEL),
            scratch_shapes=[pltpu.VMEM((tm, D_MODEL), F32),
                            pltpu.VMEM((packed_rows, D_MODEL), F32),
                            pltpu.VMEM((packed_rows, D_MODEL), F32)]),
        out_shape=jax.ShapeDtypeStruct((n, D_MODEL), F32),
        compiler_params=pltpu.CompilerParams(dimension_semantics=("arbitrary", "arbitrary"),
                                             vmem_limit_bytes=VMEM_LIMIT),
        name="moe",
    )(counts, hn, x1, comb, slot, slot_t, wg, wu, wd, gain_final)


def _lane_row(vec, offset):
    return jnp.zeros((1, LANES), F32).at[0, offset:offset + vec.shape[0]].set(vec.astype(F32))


def _block_diag_halves(w):
    g, a, b = w.shape
    hg = g // 2
    w = w.reshape(2, hg, a, b)
    eye = jnp.eye(hg, dtype=w.dtype)
    return jnp.einsum("sgab,gh->sgahb", w, eye).reshape(2, hg * a, hg * b)


MATMUL_WEIGHTS = ("w_in", "bb_re", "bb_im", "cc_re", "cc_im", "w_glu", "w_out")


def _layer_params(l, a):
    w_in = a["w_in"][l]
    w_cat = jnp.concatenate([w_in[:, :QKV_DIM + W_A], w_in[:, QKV_DIM + W_A + 2 * H_A:],
                             w_in[:, QKV_DIM + W_A:QKV_DIM + W_A + 2 * H_A],
                             jnp.zeros((D_MODEL, LANES - 2 * H_A), F32)], axis=1)
    lam_re = a["ssm_a_re"][l]
    lam_im = a["ssm_a_im"][l]
    delta = jnp.exp(a["ssm_log_dt"][l])[:, None]
    mag = jnp.exp(lam_re * delta)
    ab_re = mag * jnp.cos(lam_im * delta)
    ab_im = mag * jnp.sin(lam_im * delta)
    den = lam_re * lam_re + lam_im * lam_im
    f_re = ((ab_re - 1.0) * lam_re + ab_im * lam_im) / den
    f_im = (ab_im * lam_re - (ab_re - 1.0) * lam_im) / den
    b_re = a["ssm_b_re"][l]
    b_im = a["ssm_b_im"][l]
    bb_re = f_re[..., None] * b_re - f_im[..., None] * b_im
    bb_im = f_re[..., None] * b_im + f_im[..., None] * b_re
    return dict(
        norm_mix=a["norm_mix"][l][None], w_in=w_cat,
        conv_w=a["conv_w"][l],
        a_log_row=_lane_row(a["a_log"][l], H_A), dt_bias_row=_lane_row(a["dt_bias"][l], H_A),
        norm_gate=a["norm_gate"][l][None],
        a_re=ab_re.reshape(1, SSM_STATE), a_im=ab_im.reshape(1, SSM_STATE),
        bb_re=_block_diag_halves(jnp.swapaxes(bb_re, 1, 2)),
        bb_im=_block_diag_halves(jnp.swapaxes(bb_im, 1, 2)),
        cc_re=_block_diag_halves(jnp.swapaxes(a["ssm_c_re"][l], 1, 2)),
        cc_im=_block_diag_halves(jnp.swapaxes(a["ssm_c_im"][l], 1, 2)),
        ssm_d=a["ssm_d"][l][None], w_glu=a["w_glu"][l], b_glu=a["b_glu"][l][None],
        norm_ssm=a["norm_ssm"][l][None],
        w_out=a["w_out"][l], norm_ffn=a["norm_ffn"][l][None])


def _trunk(x, states, layers, ffn, moe, l, tm, mm):
    b, t, _ = x.shape
    n = b * t
    xf = x.reshape(n, D_MODEL)
    new_states = []
    for li, p in enumerate(layers):
        s0, c0, hr0, hi0 = states[li]
        mixed, s1, c1, hr1, hi1 = _mixer(xf.reshape(b, t, D_MODEL), s0, c0, hr0.reshape(b, SSM_STATE),
                                         hi0.reshape(b, SSM_STATE), p, l, mm)
        new_states.append((s1, c1, hr1.reshape(b, G_B, P_STATE), hi1.reshape(b, G_B, P_STATE)))
        mixed = mixed.reshape(n, D_MODEL)
        if li == 0:
            xf = _out_ffn(mixed, xf, p["w_out"], p["norm_ffn"], ffn["wg"], ffn["wu"], ffn["wd"], tm, mm)
        else:
            x1, hn, comb, slot, slot_t = _out_router(mixed, xf, p["w_out"], p["norm_ffn"], ffn["router"], tm, mm)
            xf = _moe(hn, x1, comb, slot, slot_t, moe["wg"], moe["wu"], moe["wd"], moe["norm_final"],
                      min(MOE_TM, n), _Matmul(precise=False))
    return xf.reshape(b, t, D_MODEL), new_states


def kernel(x_prompt, x_sample, state_delta, state_conv, state_ssm_re, state_ssm_im, meta_tokens, norm_mix, w_in, conv_w, a_log, dt_bias, norm_gate, ssm_a_re, ssm_a_im, ssm_b_re, ssm_b_im, ssm_c_re, ssm_c_im, ssm_d, ssm_log_dt, w_glu, b_glu, norm_ssm, w_out, norm_ffn, ffn_w_gate, ffn_w_up, ffn_w_down, router_w, moe_w_gate, moe_w_up, moe_w_down, norm_final):
    a = dict(norm_mix=norm_mix, w_in=w_in, conv_w=conv_w, a_log=a_log, dt_bias=dt_bias, norm_gate=norm_gate,
             ssm_a_re=ssm_a_re, ssm_a_im=ssm_a_im, ssm_b_re=ssm_b_re, ssm_b_im=ssm_b_im,
             ssm_c_re=ssm_c_re, ssm_c_im=ssm_c_im, ssm_d=ssm_d, ssm_log_dt=ssm_log_dt,
             w_glu=w_glu, b_glu=b_glu, norm_ssm=norm_ssm, w_out=w_out, norm_ffn=norm_ffn)
    depth = w_in.shape[0]

    layers32 = [_layer_params(l, a) for l in range(depth)]
    ffn32 = dict(wg=ffn_w_gate[0], wu=ffn_w_up[0], wd=ffn_w_down[0],
                 router=jnp.pad(router_w[0], ((0, 0), (0, LANES - N_EXPERTS))))
    moe = dict(wg=moe_w_gate[0].astype(BF16), wu=moe_w_up[0].astype(BF16), wd=moe_w_down[0].astype(BF16),
               norm_final=norm_final[None])

    def weights(mm):
        layers = [{k: (v.astype(mm.act) if k in MATMUL_WEIGHTS else v) for k, v in p.items()} for p in layers32]
        return layers, {k: v.astype(mm.act) for k, v in ffn32.items()}, moe

    bp = x_prompt.shape[0]
    bs, ts = x_sample.shape[0], x_sample.shape[1]
    side_b = 2 * MIX_BT
    pad = side_b - bs - 1
    x_side = jnp.concatenate([x_sample, meta_tokens[None], jnp.zeros((pad, ts, D_MODEL), F32)], axis=0)

    def side_state(st):
        zeros = jnp.zeros((side_b - bs,) + st.shape[1:], F32)
        return jnp.concatenate([st, zeros], axis=0)

    side_states = [(side_state(state_delta[l]), side_state(state_conv[l]),
                    side_state(state_ssm_re[l]), side_state(state_ssm_im[l])) for l in range(depth)]
    precise = _Matmul(precise=True)
    y_side, side_new = _trunk(x_side, side_states, *weights(precise), ts, side_b * ts, precise)

    def from_meta(st):
        return jnp.broadcast_to(st[bs:bs + 1], (bp,) + st.shape[1:])

    main_states = [tuple(from_meta(st) for st in side_new[l]) for l in range(depth)]
    fast = _Matmul(precise=False)
    y_prompt, main_new = _trunk(x_prompt, main_states, *weights(fast), CHUNK, 512, fast)

    def stack(new, idx, count):
        return jnp.stack([new[l][idx][:count] for l in range(depth)])

    return (y_prompt, y_side[:bs],
            stack(main_new, 0, bp), stack(main_new, 1, bp), stack(main_new, 2, bp), stack(main_new, 3, bp),
            stack(side_new, 0, bs), stack(side_new, 1, bs), stack(side_new, 2, bs), stack(side_new, 3, bs))
```

```python
import functools

import jax
import jax.numpy as jnp
from jax import lax
from jax.experimental import pallas as pl
from jax.experimental.pallas import tpu as pltpu

F32 = jnp.float32
BF16 = jnp.bfloat16

D_MODEL = 1024
N_META = 16
H_A = 4
DK = 128
DV = 128
W_A = H_A * DV
K_CONV = 4
QKV_DIM = 2 * H_A * DK + H_A * DV
W_B = D_MODEL - W_A
GROUP_CH = 16
G_B = W_B // GROUP_CH
P_STATE = 64
SSM_STATE = G_B * P_STATE
D_FF = 2816
N_EXPERTS = 8
D_FF_EXPERT = 1408
EPS = 1e-6

LANES = 128
SUBLANES = 8
PROJ_PAD = QKV_DIM + W_A + W_B + LANES
COL_Z = QKV_DIM
COL_U = QKV_DIM + W_A
COL_BA = QKV_DIM + W_A + W_B
CONV_PAD = SUBLANES
SSM_HALF = W_B // 2
SSM_HALF_STATE = SSM_STATE // 2
MIX_BT = 8
CHUNK = 64
MOE_TM = 1024
MOE_SUB = LANES
MOE_CHUNK = 128
MOE_ALIGN = SUBLANES
VMEM_LIMIT = 60 * 1024 * 1024


class _Matmul:
    def __init__(self, precise):
        self.precise = precise
        self.act = F32 if precise else BF16

    def _args(self, a, b):
        if self.precise:
            return (a.astype(F32), b.astype(F32)), dict(precision=lax.Precision.HIGHEST)
        return (a.astype(BF16), b.astype(BF16)), {}

    def dot(self, a, b):
        args, kw = self._args(a, b)
        return jnp.dot(*args, preferred_element_type=F32, **kw)

    def einsum(self, spec, a, b):
        args, kw = self._args(a, b)
        return jnp.einsum(spec, *args, preferred_element_type=F32, **kw)


def _sigmoid(x):
    return 1.0 / (1.0 + jnp.exp(-x))


def _silu(x):
    return x * _sigmoid(x)


def _rms(x, gain):
    return x * lax.rsqrt(jnp.mean(x * x, axis=-1, keepdims=True) + EPS) * gain


def _resident(shape, grid_rank):
    zeros = (0,) * len(shape)
    index_map = (lambda i: zeros) if grid_rank == 1 else (lambda i, j: zeros)
    return pl.BlockSpec(shape, index_map, pipeline_mode=pl.Buffered(1))


def _delta_head(q, k, v, beta, gcum, s, mm):
    n, l, _ = q.shape
    p = n // 2

    def side2(x):
        return jnp.concatenate([x[:p], x[p:]], axis=-1)

    def diag2(x):
        zero = jnp.zeros_like(x[:p])
        return jnp.concatenate([jnp.concatenate([x[:p], zero], axis=-1),
                                jnp.concatenate([zero, x[p:]], axis=-1)], axis=1)

    def unside2(x2, w):
        return jnp.concatenate([x2[..., :w], x2[..., w:]], axis=0)

    first = lax.broadcasted_iota(jnp.int32, (p, l, 2 * l), 2) < l

    def per_row2(x):
        return jnp.where(first, x[:p], x[p:])

    def own_block2(x2):
        return jnp.concatenate([jnp.where(first, x2, 0.0), jnp.where(first, 0.0, x2)], axis=1)

    g_hi = gcum.astype(BF16).astype(F32)
    r1 = gcum - g_hi
    g_mid = r1.astype(BF16).astype(F32)
    g_lo = r1 - g_mid
    lane = lax.broadcasted_iota(jnp.int32, (n, l, LANES), 2)
    pieces = jnp.where(lane == 0, g_hi, jnp.where(lane == 1, g_mid, jnp.where(lane == 2, g_lo, 0.0)))
    ones = jnp.ones((p, l, 2 * LANES), F32)
    g_row = mm.einsum("bik,bjk->bij", ones, diag2(pieces))
    g_col = per_row2(gcum)
    ii = lax.broadcasted_iota(jnp.int32, (p, l, 2 * l), 1)
    jj = lax.broadcasted_iota(jnp.int32, (p, l, 2 * l), 2)
    jj = jnp.where(first, jj, jj - l)
    incl = ii >= jj
    strict = ii > jj
    decay = jnp.where(incl, jnp.exp(jnp.where(incl, g_col - g_row, 0.0)), 0.0)
    k_diag = diag2(k)
    kk = mm.einsum("bik,bjk->bij", side2(k), k_diag)
    qk = mm.einsum("bik,bjk->bij", side2(q), k_diag) * decay
    neg_m = jnp.where(strict, -(per_row2(beta) * kk * decay), 0.0)
    eye = jnp.where(ii == jj, 1.0, 0.0)
    t_inv = eye + neg_m
    pw = neg_m
    size = 2
    while size < l:
        pw = mm.einsum("bij,bjk->bik", pw, own_block2(pw))
        t_inv = t_inv + mm.einsum("bij,bjk->bik", t_inv, own_block2(pw))
        size *= 2
    eg = jnp.exp(gcum)
    rhs = jnp.concatenate([beta * v, (beta * eg) * k], axis=-1)
    sol = unside2(mm.einsum("bij,bjd->bid", t_inv, diag2(rhs)), 2 * DV)
    u0 = sol[..., :DV]
    w = sol[..., DV:]
    u = u0 - mm.einsum("bik,bkv->biv", w, s)
    o = mm.einsum("bik,bkv->biv", q * eg, s) + unside2(mm.einsum("bij,bjv->biv", qk, diag2(u)), DV)
    g_last = gcum[:, l - 1:l, :]
    kg = k * jnp.exp(g_last - gcum)
    s_new = s * jnp.exp(g_last) + mm.einsum("bjk,bjv->bkv", kg, u)
    return o, s_new


def _mixer_kernel(xfirst_ref, xnext_ref, gin_ref, win_ref, s0_ref, c0_ref, hr0_ref, hi0_ref,
                  convw_ref, alog_ref, dtb_ref, ngate_ref,
                  are_ref, aim_ref, bbre_ref, bbim_ref, ccre_ref, ccim_ref,
                  d_ref, wglu_ref, bglu_ref, nssm_ref,
                  out_ref, s_ref, c_ref, hr_ref, hi_ref,
                  proj_scr, xp_scr, utb_scr, xr_scr, xi_scr, ytb_scr, *, bt, l, lookahead, mm):
    rows = bt * l
    chunk = pl.program_id(1)

    def project(x_ref):
        hn = _rms(x_ref[...].reshape(rows, D_MODEL), gin_ref[...])
        return mm.dot(hn, win_ref[...])

    @pl.when(chunk == 0)
    def _():
        s_ref[...] = s0_ref[...]
        hr_ref[...] = hr0_ref[...]
        hi_ref[...] = hi0_ref[...]
        xp_scr[:, CONV_PAD - (K_CONV - 1):CONV_PAD, :] = c0_ref[...]
        proj_scr[0:rows, :] = project(xfirst_ref)

    slot = chunk % 2
    cur = pl.ds(pl.multiple_of(slot * rows, rows), rows)

    def proj_cols(start, width):
        return proj_scr[cur, start:start + width].reshape(bt, l, width)

    xp_scr[:, CONV_PAD:CONV_PAD + l, :] = proj_cols(0, QKV_DIM)
    acc = None
    for j in range(K_CONV):
        off = CONV_PAD - (K_CONV - 1) + j
        term = xp_scr[:, off:off + l, :] * convw_ref[j:j + 1, :]
        acc = term if acc is None else acc + term
    tail = xp_scr[:, CONV_PAD + l - (K_CONV - 1):CONV_PAD + l, :]
    xp_scr[:, CONV_PAD - (K_CONV - 1):CONV_PAD, :] = tail
    c_ref[...] = tail
    qkv = _silu(acc)

    ba = proj_scr[cur, COL_BA:COL_BA + LANES]
    beta_all = _sigmoid(ba)
    sp_in = ba + dtb_ref[...]
    softplus = jnp.maximum(sp_in, 0.0) + jnp.log1p(jnp.exp(-jnp.abs(sp_in)))
    g_all = -jnp.exp(alog_ref[...]) * softplus
    t_idx = lax.broadcasted_iota(jnp.int32, (rows, LANES), 0) % l
    shift = 1
    while shift < l:
        g_all = g_all + jnp.where(t_idx >= shift, pltpu.roll(g_all, shift, 0), 0.0)
        shift *= 2
    beta_all = beta_all.reshape(bt, l, LANES)
    g_all = g_all.reshape(bt, l, LANES)

    def heads(first_col, width, src):
        return jnp.concatenate([src[:, :, first_col + h * width:first_col + (h + 1) * width]
                                for h in range(H_A)], axis=0)

    z = proj_cols(COL_Z, W_A)
    q = heads(0, DK, qkv)
    k = heads(H_A * DK, DK, qkv)
    v = heads(2 * H_A * DK, DV, qkv)
    q = q * lax.rsqrt(jnp.sum(q * q, axis=-1, keepdims=True) + EPS) * (DK ** -0.5)
    k = k * lax.rsqrt(jnp.sum(k * k, axis=-1, keepdims=True) + EPS)
    s_all = jnp.concatenate([s_ref[:, h] for h in range(H_A)], axis=0)
    o, s_new = _delta_head(q, k, v, heads(0, 1, beta_all), heads(H_A, 1, g_all), s_all, mm)
    o = _rms(o, ngate_ref[...]) * _silu(heads(0, DV, z))
    for h in range(H_A):
        s_ref[:, h] = s_new[h * bt:(h + 1) * bt]
        out_ref[:, :, h * DV:(h + 1) * DV] = o[h * bt:(h + 1) * bt].astype(out_ref.dtype)

    u_bt = proj_scr[cur, COL_U:COL_U + W_B]
    for c in range(W_B // LANES):
        ytb_scr[c] = u_bt[:, c * LANES:(c + 1) * LANES]
    for t in range(l):
        for c in range(W_B // LANES):
            utb_scr[t * bt:(t + 1) * bt, c * LANES:(c + 1) * LANES] = ytb_scr[c, pl.ds(t, bt, stride=l), :]
    u_tb = utb_scr[...]
    for half in range(2):
        uh = u_tb[:, half * SSM_HALF:(half + 1) * SSM_HALF]
        cols = slice(half * SSM_HALF_STATE, (half + 1) * SSM_HALF_STATE)
        xr_scr[:, cols] = mm.dot(uh, bbre_ref[half])
        xi_scr[:, cols] = mm.dot(uh, bbim_ref[half])

    for half in range(2):
        cols = slice(half * SSM_HALF_STATE, (half + 1) * SSM_HALF_STATE)
        a_re = jnp.broadcast_to(are_ref[:, cols], (bt, SSM_HALF_STATE))
        a_im = jnp.broadcast_to(aim_ref[:, cols], (bt, SSM_HALF_STATE))

        def step(t, carry):
            h_re, h_im = carry
            r = pl.multiple_of(t * bt, bt)
            n_re = a_re * h_re - a_im * h_im + xr_scr[pl.ds(r, bt), cols]
            n_im = a_re * h_im + a_im * h_re + xi_scr[pl.ds(r, bt), cols]
            xr_scr[pl.ds(r, bt), cols] = n_re
            xi_scr[pl.ds(r, bt), cols] = n_im
            return n_re, n_im

        h_re, h_im = lax.fori_loop(0, l, step, (hr_ref[:, cols], hi_ref[:, cols]))
        hr_ref[:, cols] = h_re
        hi_ref[:, cols] = h_im

    y_parts = []
    for half in range(2):
        cols = slice(half * SSM_HALF_STATE, (half + 1) * SSM_HALF_STATE)
        y_parts.append(mm.dot(xr_scr[:, cols], ccre_ref[half]) - mm.dot(xi_scr[:, cols], ccim_ref[half]))
    y = jnp.concatenate(y_parts, axis=-1) + d_ref[...] * u_tb
    gy = 0.5 * y * (1.0 + jnp.tanh(0.7978845608028654 * (y + 0.044715 * (y * y * y))))
    ob = gy * _sigmoid(mm.dot(gy, wglu_ref[...]) + bglu_ref[...])
    ob = _rms(ob, nssm_ref[...])
    for c in range(W_B // LANES):
        ytb_scr[c] = ob[:, c * LANES:(c + 1) * LANES]
    for b in range(bt):
        for c in range(W_B // LANES):
            out_ref[b, :, W_A + c * LANES:W_A + (c + 1) * LANES] = \
                ytb_scr[c, pl.ds(b, l, stride=bt), :].astype(out_ref.dtype)

    if lookahead:
        proj_scr[pl.ds(pl.multiple_of((1 - slot) * rows, rows), rows), :] = project(xnext_ref)


def _mixer(x, s0, c0, hr0, hi0, p, l, mm):
    b, t, _ = x.shape
    bt = MIX_BT
    rows = bt * l
    nchunks = t // l
    grid = (b // bt, nchunks)

    def state(shape, **kw):
        nd = len(shape)
        return pl.BlockSpec((bt,) + shape, lambda i, n: (i,) + (0,) * nd, **kw)

    def const(shape):
        return _resident(shape, 2)

    x_first = pl.BlockSpec((bt, l, D_MODEL), lambda i, n: (i, 0, 0), pipeline_mode=pl.Buffered(1))
    x_next = pl.BlockSpec((bt, l, D_MODEL), lambda i, n: (i, jnp.minimum(n + 1, nchunks - 1), 0))
    consts = [p["norm_mix"], p["w_in"]]
    consts_tail = [p["conv_w"], p["a_log_row"], p["dt_bias_row"], p["norm_gate"],
              p["a_re"], p["a_im"], p["bb_re"], p["bb_im"], p["cc_re"], p["cc_im"],
              p["ssm_d"], p["w_glu"], p["b_glu"], p["norm_ssm"]]
    state_shapes = [(H_A, DK, DV), (K_CONV - 1, QKV_DIM), (SSM_STATE,), (SSM_STATE,)]
    out_shapes = ([jax.ShapeDtypeStruct((b, t, D_MODEL), mm.act)]
                  + [jax.ShapeDtypeStruct((b,) + s, F32) for s in state_shapes])
    return pl.pallas_call(
        functools.partial(_mixer_kernel, bt=bt, l=l, lookahead=nchunks > 1, mm=mm),
        grid=grid,
        in_specs=([x_first, x_next] + [const(c.shape) for c in consts]
                  + [state(s, pipeline_mode=pl.Buffered(1)) for s in state_shapes]
                  + [const(c.shape) for c in consts_tail]),
        out_specs=[pl.BlockSpec((bt, l, D_MODEL), lambda i, n: (i, n, 0))] + [state(s) for s in state_shapes],
        out_shape=out_shapes,
        scratch_shapes=[pltpu.VMEM((2 * rows, PROJ_PAD), F32),
                        pltpu.VMEM((bt, CONV_PAD + l, QKV_DIM), F32),
                        pltpu.VMEM((rows, W_B), F32),
                        pltpu.VMEM((rows, SSM_STATE), F32),
                        pltpu.VMEM((rows, SSM_STATE), F32),
                        pltpu.VMEM((W_B // LANES, rows, LANES), F32)],
        compiler_params=pltpu.CompilerParams(dimension_semantics=("parallel", "arbitrary"),
                                             vmem_limit_bytes=VMEM_LIMIT),
        name="mixer",
    )(x, x, *consts, s0, c0, hr0, hi0, *consts_tail)


def _out_ffn_kernel(mix_ref, x_ref, wo_ref, g_ref, wg_ref, wu_ref, wd_ref, o_ref, *, mm):
    x1 = x_ref[...] + mm.dot(mix_ref[...], wo_ref[...])
    hn = _rms(x1, g_ref[...]).astype(mm.act)
    acc = x1
    for c in range(D_FF // D_FF_EXPERT):
        cols = slice(c * D_FF_EXPERT, (c + 1) * D_FF_EXPERT)
        hidden = _silu(mm.dot(hn, wg_ref[:, cols])) * mm.dot(hn, wu_ref[:, cols])
        acc = acc + mm.dot(hidden, wd_ref[cols, :])
    o_ref[...] = acc


def _out_ffn(mixed, x, w_out, gain, wg, wu, wd, tm, mm):
    n = x.shape[0]
    row = lambda w: pl.BlockSpec((tm, w), lambda i: (i, 0))
    const = lambda a: _resident(a.shape, 1)
    return pl.pallas_call(
        functools.partial(_out_ffn_kernel, mm=mm),
        grid=(pl.cdiv(n, tm),),
        in_specs=[row(D_MODEL), row(D_MODEL), const(w_out), const(gain), const(wg), const(wu), const(wd)],
        out_specs=row(D_MODEL),
        out_shape=jax.ShapeDtypeStruct((n, D_MODEL), F32),
        compiler_params=pltpu.CompilerParams(dimension_semantics=("parallel",),
                                             vmem_limit_bytes=VMEM_LIMIT),
        name="out_ffn",
    )(mixed, x, w_out, gain, wg, wu, wd)


def _out_router_kernel(mix_ref, x_ref, wo_ref, g_ref, wr_ref, x1_ref, hn_ref, comb_ref, slot_ref, slott_ref,
                       *, mm):
    x1 = x_ref[...] + mm.dot(mix_ref[...], wo_ref[...])
    x1_ref[...] = x1
    hn = _rms(x1, g_ref[...]).astype(mm.act)
    hn_ref[...] = hn
    logits = mm.dot(hn, wr_ref[...])
    lane = lax.broadcasted_iota(jnp.int32, logits.shape, 1).astype(F32)
    neg = jnp.float32(-jnp.inf)
    logits = jnp.where(lane < N_EXPERTS, logits, neg)
    v1 = jnp.max(logits, axis=-1, keepdims=True)
    i1 = jnp.min(jnp.where(logits == v1, lane, float(LANES)), axis=-1, keepdims=True)
    rest = jnp.where(lane == i1, neg, logits)
    v2 = jnp.max(rest, axis=-1, keepdims=True)
    i2 = jnp.min(jnp.where(rest == v2, lane, float(LANES)), axis=-1, keepdims=True)
    e2 = jnp.exp(v2 - v1)
    den = 1.0 + e2
    comb = jnp.where(lane == i1, 1.0 / den, jnp.where(lane == i2, e2 / den, 0.0))
    comb_ref[...] = comb
    routed = comb > 0.0
    routed_t = routed.astype(F32).T[:N_EXPERTS, :] > 0.0
    r_i = lax.broadcasted_iota(jnp.int32, (MOE_SUB, MOE_SUB), 0)
    c_i = lax.broadcasted_iota(jnp.int32, (MOE_SUB, MOE_SUB), 1)
    earlier_rows = jnp.where(c_i < r_i, 1.0, 0.0).astype(BF16)
    earlier_cols = jnp.where(r_i < c_i, 1.0, 0.0).astype(BF16)
    for s in range(comb.shape[0] // MOE_SUB):
        rows = slice(s * MOE_SUB, (s + 1) * MOE_SUB)
        rank = jnp.dot(earlier_rows, jnp.where(routed[rows], 1.0, 0.0).astype(BF16),
                       preferred_element_type=F32)
        slot_ref[rows, :] = jnp.where(routed[rows], rank, -1.0)
        rank_t = jnp.dot(jnp.where(routed_t[:, rows], 1.0, 0.0).astype(BF16), earlier_cols,
                         preferred_element_type=F32)
        slott_ref[:, rows] = jnp.where(routed_t[:, rows], rank_t, -1.0)


def _out_router(mixed, x, w_out, gain, w_router, tm, mm):
    n = x.shape[0]
    row = lambda w: pl.BlockSpec((tm, w), lambda i: (i, 0))
    const = lambda a: _resident(a.shape, 1)
    return pl.pallas_call(
        functools.partial(_out_router_kernel, mm=mm),
        grid=(pl.cdiv(n, tm),),
        in_specs=[row(D_MODEL), row(D_MODEL), const(w_out), const(gain), const(w_router)],
        out_specs=[row(D_MODEL), row(D_MODEL), row(LANES), row(LANES),
                   pl.BlockSpec((N_EXPERTS, tm), lambda i: (0, i))],
        out_shape=[jax.ShapeDtypeStruct((n, D_MODEL), F32),
                   jax.ShapeDtypeStruct((n, D_MODEL), mm.act),
                   jax.ShapeDtypeStruct((n, LANES), F32),
                   jax.ShapeDtypeStruct((n, LANES), F32),
                   jax.ShapeDtypeStruct((N_EXPERTS, n), F32)],
        compiler_params=pltpu.CompilerParams(dimension_semantics=("parallel",),
                                             vmem_limit_bytes=VMEM_LIMIT),
        name="out_router",
    )(mixed, x, w_out, gain, w_router)


def _moe_kernel(cnt_ref, hn_ref, x1_ref, comb_ref, slot_ref, slott_ref, wg_ref, wu_ref, wd_ref, gf_ref, o_ref,
                acc_ref, xc_ref, y_ref, *, tm, mm):
    i = pl.program_id(0)
    e = pl.program_id(1)
    nsub = tm // MOE_SUB

    @pl.when((i == 0) & (e == 0))
    def _():
        xc_ref[...] = jnp.zeros_like(xc_ref)
        y_ref[...] = jnp.zeros_like(y_ref)

    @pl.when(e == 0)
    def _():
        acc_ref[...] = jnp.zeros_like(acc_ref)

    lane = lax.broadcasted_iota(jnp.int32, (MOE_SUB, LANES), 1)
    expert_row = lax.broadcasted_iota(jnp.int32, (N_EXPERTS, MOE_SUB), 0) == e

    spans = []
    base = 0
    for s in range(nsub):
        cnt = cnt_ref[(i * nsub + s) * N_EXPERTS + e]
        spans.append(base)
        base = base + ((cnt + MOE_ALIGN - 1) // MOE_ALIGN) * MOE_ALIGN
    total = base

    slot_row = lax.broadcasted_iota(jnp.int32, (MOE_SUB, MOE_SUB), 0).astype(F32)
    for s, base_s in enumerate(spans):
        rows = slice(s * MOE_SUB, (s + 1) * MOE_SUB)
        slot = jnp.sum(jnp.where(expert_row, slott_ref[:, rows], 0.0), axis=0, keepdims=True)
        onehot = jnp.where(slot == slot_row, 1.0, 0.0).astype(BF16)
        dst = pl.ds(pl.multiple_of(base_s, MOE_ALIGN), MOE_SUB)
        xc_ref[dst, :] = mm.dot(onehot, hn_ref[rows, :])

    def ffn(c, carry):
        r = pl.ds(pl.multiple_of(c * MOE_CHUNK, MOE_CHUNK), MOE_CHUNK)
        x = xc_ref[r, :].astype(mm.act)
        hidden = _silu(mm.dot(x, wg_ref[0])) * mm.dot(x, wu_ref[0])
        y_ref[r, :] = mm.dot(hidden, wd_ref[0])
        return carry

    lax.fori_loop(0, (total + MOE_CHUNK - 1) // MOE_CHUNK, ffn, 0)

    pieces = 1 if mm.precise else 2
    slot_lane = lax.broadcasted_iota(jnp.int32, (MOE_SUB, pieces * MOE_SUB), 1)
    slot_lane = jnp.where(slot_lane >= MOE_SUB, slot_lane - MOE_SUB, slot_lane).astype(F32)
    for s, base_s in enumerate(spans):
        rows = slice(s * MOE_SUB, (s + 1) * MOE_SUB)
        gate = jnp.sum(jnp.where(lane == e, comb_ref[rows, :], 0.0), axis=-1, keepdims=True)
        slot = jnp.sum(jnp.where(lane == e, slot_ref[rows, :], 0.0), axis=-1, keepdims=True)
        y = y_ref[pl.ds(pl.multiple_of(base_s, MOE_ALIGN), MOE_SUB), :]
        if not mm.precise:
            y_hi = y.astype(BF16)
            y = jnp.concatenate([y_hi, (y - y_hi.astype(F32)).astype(BF16)], axis=0)
        onehot = jnp.where(slot == slot_lane, 1.0, 0.0)
        back = mm.dot(onehot, y)
        acc_ref[rows, :] += gate * back

    @pl.when(e == N_EXPERTS - 1)
    def _():
        o_ref[...] = _rms(x1_ref[...] + acc_ref[...], gf_ref[...])


def _moe(hn, x1, comb, slot, slot_t, wg, wu, wd, gain_final, tm, mm):
    n = x1.shape[0]
    assert n % tm == 0 and tm % MOE_SUB == 0
    routed = comb[:, :N_EXPERTS] > 0.0
    counts = jnp.sum(routed.reshape(n // MOE_SUB, MOE_SUB, N_EXPERTS), axis=1, dtype=jnp.int32).reshape(-1)
    row = lambda w: pl.BlockSpec((tm, w), lambda i, e, cnt: (i, 0))
    expert = lambda a, b: pl.BlockSpec((1, a, b), lambda i, e, cnt: (e, 0, 0))
    packed_rows = tm + max(MOE_CHUNK, MOE_SUB) + (tm // MOE_SUB) * MOE_ALIGN
    return pl.pallas_call(
        functools.partial(_moe_kernel, tm=tm, mm=mm),
        grid_spec=pltpu.PrefetchScalarGridSpec(
            num_scalar_prefetch=1,
            grid=(n // tm, N_EXPERTS),
            in_specs=[row(D_MODEL), row(D_MODEL), row(LANES), row(LANES),
                      pl.BlockSpec((N_EXPERTS, tm), lambda i, e, cnt: (0, i)),
                      expert(D_MODEL, D_FF_EXPERT), expert(D_MODEL, D_FF_EXPERT), expert(D_FF_EXPERT, D_MODEL),
                      pl.BlockSpec((1, D_MODEL), lambda i, e, cnt: (0, 0))],
            out_specs=row(D_MODEL),
            scratch_shapes=[pltpu.VMEM((tm, D_MODEL), F32),
                            pltpu.VMEM((packed_rows, D_MODEL), F32),
                            pltpu.VMEM((packed_rows, D_MODEL), F32)]),
        out_shape=jax.ShapeDtypeStruct((n, D_MODEL), F32),
        compiler_params=pltpu.CompilerParams(dimension_semantics=("arbitrary", "arbitrary"),
                                             vmem_limit_bytes=VMEM_LIMIT),
        name="moe",
    )(counts, hn, x1, comb, slot, slot_t, wg, wu, wd, gain_final)


def _lane_row(vec, offset):
    return jnp.zeros((1, LANES), F32).at[0, offset:offset + vec.shape[0]].set(vec.astype(F32))


def _block_diag_halves(w):
    g, a, b = w.shape
    hg = g // 2
    w = w.reshape(2, hg, a, b)
    eye = jnp.eye(hg, dtype=w.dtype)
    return jnp.einsum("sgab,gh->sgahb", w, eye).reshape(2, hg * a, hg * b)


MATMUL_WEIGHTS = ("w_in", "bb_re", "bb_im", "cc_re", "cc_im", "w_glu", "w_out")


def _layer_params(l, a):
    w_in = a["w_in"][l]
    w_cat = jnp.concatenate([w_in[:, :QKV_DIM + W_A], w_in[:, QKV_DIM + W_A + 2 * H_A:],
                             w_in[:, QKV_DIM + W_A:QKV_DIM + W_A + 2 * H_A],
                             jnp.zeros((D_MODEL, LANES - 2 * H_A), F32)], axis=1)
    lam_re = a["ssm_a_re"][l]
    lam_im = a["ssm_a_im"][l]
    delta = jnp.exp(a["ssm_log_dt"][l])[:, None]
    mag = jnp.exp(lam_re * delta)
    ab_re = mag * jnp.cos(lam_im * delta)
    ab_im = mag * jnp.sin(lam_im * delta)
    den = lam_re * lam_re + lam_im * lam_im
    f_re = ((ab_re - 1.0) * lam_re + ab_im * lam_im) / den
    f_im = (ab_im * lam_re - (ab_re - 1.0) * lam_im) / den
    b_re = a["ssm_b_re"][l]
    b_im = a["ssm_b_im"][l]
    bb_re = f_re[..., None] * b_re - f_im[..., None] * b_im
    bb_im = f_re[..., None] * b_im + f_im[..., None] * b_re
    return dict(
        norm_mix=a["norm_mix"][l][None], w_in=w_cat,
        conv_w=a["conv_w"][l],
        a_log_row=_lane_row(a["a_log"][l], H_A), dt_bias_row=_lane_row(a["dt_bias"][l], H_A),
        norm_gate=a["norm_gate"][l][None],
        a_re=ab_re.reshape(1, SSM_STATE), a_im=ab_im.reshape(1, SSM_STATE),
        bb_re=_block_diag_halves(jnp.swapaxes(bb_re, 1, 2)),
        bb_im=_block_diag_halves(jnp.swapaxes(bb_im, 1, 2)),
        cc_re=_block_diag_halves(jnp.swapaxes(a["ssm_c_re"][l], 1, 2)),
        cc_im=_block_diag_halves(jnp.swapaxes(a["ssm_c_im"][l], 1, 2)),
        ssm_d=a["ssm_d"][l][None], w_glu=a["w_glu"][l], b_glu=a["b_glu"][l][None],
        norm_ssm=a["norm_ssm"][l][None],
        w_out=a["w_out"][l], norm_ffn=a["norm_ffn"][l][None])


def _trunk(x, states, layers, ffn, moe, l, tm, mm):
    b, t, _ = x.shape
    n = b * t
    xf = x.reshape(n, D_MODEL)
    new_states = []
    for li, p in enumerate(layers):
        s0, c0, hr0, hi0 = states[li]
        mixed, s1, c1, hr1, hi1 = _mixer(xf.reshape(b, t, D_MODEL), s0, c0, hr0.reshape(b, SSM_STATE),
                                         hi0.reshape(b, SSM_STATE), p, l, mm)
        new_states.append((s1, c1, hr1.reshape(b, G_B, P_STATE), hi1.reshape(b, G_B, P_STATE)))
        mixed = mixed.reshape(n, D_MODEL)
        if li == 0:
            xf = _out_ffn(mixed, xf, p["w_out"], p["norm_ffn"], ffn["wg"], ffn["wu"], ffn["wd"], tm, mm)
        else:
            x1, hn, comb, slot, slot_t = _out_router(mixed, xf, p["w_out"], p["norm_ffn"], ffn["router"], tm, mm)
            xf = _moe(hn, x1, comb, slot, slot_t, moe["wg"], moe["wu"], moe["wd"], moe["norm_final"],
                      min(MOE_TM, n), _Matmul(precise=False))
    return xf.reshape(b, t, D_MODEL), new_states


def kernel(x_prompt, x_sample, state_delta, state_conv, state_ssm_re, state_ssm_im, meta_tokens, norm_mix, w_in, conv_w, a_log, dt_bias, norm_gate, ssm_a_re, ssm_a_im, ssm_b_re, ssm_b_im, ssm_c_re, ssm_c_im, ssm_d, ssm_log_dt, w_glu, b_glu, norm_ssm, w_out, norm_ffn, ffn_w_gate, ffn_w_up, ffn_w_down, router_w, moe_w_gate, moe_w_up, moe_w_down, norm_final):
    a = dict(norm_mix=norm_mix, w_in=w_in, conv_w=conv_w, a_log=a_log, dt_bias=dt_bias, norm_gate=norm_gate,
             ssm_a_re=ssm_a_re, ssm_a_im=ssm_a_im, ssm_b_re=ssm_b_re, ssm_b_im=ssm_b_im,
             ssm_c_re=ssm_c_re, ssm_c_im=ssm_c_im, ssm_d=ssm_d, ssm_log_dt=ssm_log_dt,
             w_glu=w_glu, b_glu=b_glu, norm_ssm=norm_ssm, w_out=w_out, norm_ffn=norm_ffn)
    depth = w_in.shape[0]

    layers32 = [_layer_params(l, a) for l in range(depth)]
    ffn32 = dict(wg=ffn_w_gate[0], wu=ffn_w_up[0], wd=ffn_w_down[0],
                 router=jnp.pad(router_w[0], ((0, 0), (0, LANES - N_EXPERTS))))
    moe = dict(wg=moe_w_gate[0].astype(BF16), wu=moe_w_up[0].astype(BF16), wd=moe_w_down[0].astype(BF16),
               norm_final=norm_final[None])

    def weights(mm):
        layers = [{k: (v.astype(mm.act) if k in MATMUL_WEIGHTS else v) for k, v in p.items()} for p in layers32]
        return layers, {k: v.astype(mm.act) for k, v in ffn32.items()}, moe

    bp = x_prompt.shape[0]
    bs, ts = x_sample.shape[0], x_sample.shape[1]
    side_b = 2 * MIX_BT
    pad = side_b - bs - 1
    x_side = jnp.concatenate([x_sample, meta_tokens[None], jnp.zeros((pad, ts, D_MODEL), F32)], axis=0)

    def side_state(st):
        zeros = jnp.zeros((side_b - bs,) + st.shape[1:], F32)
        return jnp.concatenate([st, zeros], axis=0)

    side_states = [(side_state(state_delta[l]), side_state(state_conv[l]),
                    side_state(state_ssm_re[l]), side_state(state_ssm_im[l])) for l in range(depth)]
    precise = _Matmul(precise=True)
    y_side, side_new = _trunk(x_side, side_states, *weights(precise), ts, side_b * ts, precise)

    def from_meta(st):
        return jnp.broadcast_to(st[bs:bs + 1], (bp,) + st.shape[1:])

    main_states = [tuple(from_meta(st) for st in side_new[l]) for l in range(depth)]
    fast = _Matmul(precise=False)
    y_prompt, main_new = _trunk(x_prompt, main_states, *weights(fast), CHUNK, 512, fast)

    def stack(new, idx, count):
        return jnp.stack([new[l][idx][:count] for l in range(depth)])

    return (y_prompt, y_side[:bs],
            stack(main_new, 0, bp), stack(main_new, 1, bp), stack(main_new, 2, bp), stack(main_new, 3, bp),
            stack(side_new, 0, bs), stack(side_new, 1, bs), stack(side_new, 2, bs), stack(side_new, 3, bs))
```

```python
import functools

import jax
import jax.numpy as jnp
from jax import lax
from jax.experimental import pallas as pl
from jax.experimental.pallas import tpu as pltpu

F32 = jnp.float32
BF16 = jnp.bfloat16

D_MODEL = 1024
N_META = 16
H_A = 4
DK = 128
DV = 128
W_A = H_A * DV
K_CONV = 4
QKV_DIM = 2 * H_A * DK + H_A * DV
W_B = D_MODEL - W_A
GROUP_CH = 16
G_B = W_B // GROUP_CH
P_STATE = 64
SSM_STATE = G_B * P_STATE
D_FF = 2816
N_EXPERTS = 8
D_FF_EXPERT = 1408
EPS = 1e-6

LANES = 128
SUBLANES = 8
PROJ_PAD = QKV_DIM + W_A + W_B + LANES
COL_Z = QKV_DIM
COL_U = QKV_DIM + W_A
COL_BA = QKV_DIM + W_A + W_B
CONV_PAD = SUBLANES
SSM_HALF = W_B // 2
SSM_HALF_STATE = SSM_STATE // 2
MIX_BT = 8
CHUNK = 64
MOE_TM = 1024
MOE_SUB = LANES
MOE_CHUNK = 128
MOE_ALIGN = SUBLANES
VMEM_LIMIT = 60 * 1024 * 1024


class _Matmul:
    def __init__(self, precise):
        self.precise = precise
        self.act = F32 if precise else BF16

    def _args(self, a, b):
        if self.precise:
            return (a.astype(F32), b.astype(F32)), dict(precision=lax.Precision.HIGHEST)
        return (a.astype(BF16), b.astype(BF16)), {}

    def dot(self, a, b):
        args, kw = self._args(a, b)
        return jnp.dot(*args, preferred_element_type=F32, **kw)

    def einsum(self, spec, a, b):
        args, kw = self._args(a, b)
        return jnp.einsum(spec, *args, preferred_element_type=F32, **kw)


def _sigmoid(x):
    return 1.0 / (1.0 + jnp.exp(-x))


def _silu(x):
    return x * _sigmoid(x)


def _rms(x, gain):
    return x * lax.rsqrt(jnp.mean(x * x, axis=-1, keepdims=True) + EPS) * gain


def _resident(shape, grid_rank):
    zeros = (0,) * len(shape)
    index_map = (lambda i: zeros) if grid_rank == 1 else (lambda i, j: zeros)
    return pl.BlockSpec(shape, index_map, pipeline_mode=pl.Buffered(1))


def _delta_head(q, k, v, beta, gcum, s, mm):
    n, l, _ = q.shape
    p = n // 2

    def side2(x):
        return jnp.concatenate([x[:p], x[p:]], axis=-1)

    def diag2(x):
        zero = jnp.zeros_like(x[:p])
        return jnp.concatenate([jnp.concatenate([x[:p], zero], axis=-1),
                                jnp.concatenate([zero, x[p:]], axis=-1)], axis=1)

    def unside2(x2, w):
        return jnp.concatenate([x2[..., :w], x2[..., w:]], axis=0)

    first = lax.broadcasted_iota(jnp.int32, (p, l, 2 * l), 2) < l

    def per_row2(x):
        return jnp.where(first, x[:p], x[p:])

    def own_block2(x2):
        return jnp.concatenate([jnp.where(first, x2, 0.0), jnp.where(first, 0.0, x2)], axis=1)

    g_hi = gcum.astype(BF16).astype(F32)
    r1 = gcum - g_hi
    g_mid = r1.astype(BF16).astype(F32)
    g_lo = r1 - g_mid
    lane = lax.broadcasted_iota(jnp.int32, (n, l, LANES), 2)
    pieces = jnp.where(lane == 0, g_hi, jnp.where(lane == 1, g_mid, jnp.where(lane == 2, g_lo, 0.0)))
    ones = jnp.ones((p, l, 2 * LANES), F32)
    g_row = mm.einsum("bik,bjk->bij", ones, diag2(pieces))
    g_col = per_row2(gcum)
    ii = lax.broadcasted_iota(jnp.int32, (p, l, 2 * l), 1)
    jj = lax.broadcasted_iota(jnp.int32, (p, l, 2 * l), 2)
    jj = jnp.where(first, jj, jj - l)
    incl = ii >= jj
    strict = ii > jj
    decay = jnp.where(incl, jnp.exp(jnp.where(incl, g_col - g_row, 0.0)), 0.0)
    k_diag = diag2(k)
    kk = mm.einsum("bik,bjk->bij", side2(k), k_diag)
    qk = mm.einsum("bik,bjk->bij", side2(q), k_diag) * decay
    neg_m = jnp.where(strict, -(per_row2(beta) * kk * decay), 0.0)
    eye = jnp.where(ii == jj, 1.0, 0.0)
    t_inv = eye + neg_m
    pw = neg_m
    size = 2
    while size < l:
        pw = mm.einsum("bij,bjk->bik", pw, own_block2(pw))
        t_inv = t_inv + mm.einsum("bij,bjk->bik", t_inv, own_block2(pw))
        size *= 2
    eg = jnp.exp(gcum)
    rhs = jnp.concatenate([beta * v, (beta * eg) * k], axis=-1)
    sol = unside2(mm.einsum("bij,bjd->bid", t_inv, diag2(rhs)), 2 * DV)
    u0 = sol[..., :DV]
    w = sol[..., DV:]
    u = u0 - mm.einsum("bik,bkv->biv", w, s)
    o = mm.einsum("bik,bkv->biv", q * eg, s) + unside2(mm.einsum("bij,bjv->biv", qk, diag2(u)), DV)
    g_last = gcum[:, l - 1:l, :]
    kg = k * jnp.exp(g_last - gcum)
    s_new = s * jnp.exp(g_last) + mm.einsum("bjk,bjv->bkv", kg, u)
    return o, s_new


def _mixer_kernel(xfirst_ref, xnext_ref, gin_ref, win_ref, s0_ref, c0_ref, hr0_ref, hi0_ref,
                  convw_ref, alog_ref, dtb_ref, ngate_ref,
                  are_ref, aim_ref, bbre_ref, bbim_ref, ccre_ref, ccim_ref,
                  d_ref, wglu_ref, bglu_ref, nssm_ref,
                  out_ref, s_ref, c_ref, hr_ref, hi_ref,
                  proj_scr, xp_scr, xr_scr, xi_scr, *, bt, l, lookahead, mm):
    rows = bt * l
    chunk = pl.program_id(1)

    def project(x_ref):
        hn = _rms(x_ref[...].reshape(rows, D_MODEL), gin_ref[...])
        return mm.dot(hn, win_ref[...])

    @pl.when(chunk == 0)
    def _():
        s_ref[...] = s0_ref[...]
        hr_ref[...] = hr0_ref[...]
        hi_ref[...] = hi0_ref[...]
        xp_scr[:, CONV_PAD - (K_CONV - 1):CONV_PAD, :] = c0_ref[...]
        proj_scr[0:rows, :] = project(xfirst_ref)

    slot = chunk % 2
    cur = pl.ds(pl.multiple_of(slot * rows, rows), rows)

    def proj_cols(start, width):
        return proj_scr[cur, start:start + width].reshape(bt, l, width)

    xp_scr[:, CONV_PAD:CONV_PAD + l, :] = proj_cols(0, QKV_DIM)
    xp = xp_scr[...]
    acc = xp[:, CONV_PAD:, :] * convw_ref[K_CONV - 1:K_CONV, :]
    for back in range(1, K_CONV):
        shifted = pltpu.roll(xp, back, 1)[:, CONV_PAD:, :]
        acc = acc + shifted * convw_ref[K_CONV - 1 - back:K_CONV - back, :]
    tail = xp_scr[:, CONV_PAD + l - (K_CONV - 1):CONV_PAD + l, :]
    xp_scr[:, CONV_PAD - (K_CONV - 1):CONV_PAD, :] = tail
    c_ref[...] = tail
    qkv = _silu(acc)

    ba = proj_scr[cur, COL_BA:COL_BA + LANES]
    beta_all = _sigmoid(ba)
    sp_in = ba + dtb_ref[...]
    softplus = jnp.maximum(sp_in, 0.0) + jnp.log1p(jnp.exp(-jnp.abs(sp_in)))
    g_all = -jnp.exp(alog_ref[...]) * softplus
    t_idx = lax.broadcasted_iota(jnp.int32, (rows, LANES), 0) % l
    shift = 1
    while shift < l:
        g_all = g_all + jnp.where(t_idx >= shift, pltpu.roll(g_all, shift, 0), 0.0)
        shift *= 2
    beta_all = beta_all.reshape(bt, l, LANES)
    g_all = g_all.reshape(bt, l, LANES)

    def heads(first_col, width, src):
        return jnp.concatenate([src[:, :, first_col + h * width:first_col + (h + 1) * width]
                                for h in range(H_A)], axis=0)

    z = proj_cols(COL_Z, W_A)
    q = heads(0, DK, qkv)
    k = heads(H_A * DK, DK, qkv)
    v = heads(2 * H_A * DK, DV, qkv)
    q = q * lax.rsqrt(jnp.sum(q * q, axis=-1, keepdims=True) + EPS) * (DK ** -0.5)
    k = k * lax.rsqrt(jnp.sum(k * k, axis=-1, keepdims=True) + EPS)
    s_all = jnp.concatenate([s_ref[:, h] for h in range(H_A)], axis=0)
    o, s_new = _delta_head(q, k, v, heads(0, 1, beta_all), heads(H_A, 1, g_all), s_all, mm)
    o = _rms(o, ngate_ref[...]) * _silu(heads(0, DV, z))
    for h in range(H_A):
        s_ref[:, h] = s_new[h * bt:(h + 1) * bt]
        out_ref[:, :, h * DV:(h + 1) * DV] = o[h * bt:(h + 1) * bt].astype(out_ref.dtype)

    u_tb = pltpu.einshape("blc->lbc", proj_cols(COL_U, W_B)).reshape(rows, W_B)
    for half in range(2):
        uh = u_tb[:, half * SSM_HALF:(half + 1) * SSM_HALF]
        cols = slice(half * SSM_HALF_STATE, (half + 1) * SSM_HALF_STATE)
        xr_scr[:, cols] = mm.dot(uh, bbre_ref[half])
        xi_scr[:, cols] = mm.dot(uh, bbim_ref[half])

    for half in range(2):
        cols = slice(half * SSM_HALF_STATE, (half + 1) * SSM_HALF_STATE)
        a_re = jnp.broadcast_to(are_ref[:, cols], (bt, SSM_HALF_STATE))
        a_im = jnp.broadcast_to(aim_ref[:, cols], (bt, SSM_HALF_STATE))

        def step(t, carry):
            h_re, h_im = carry
            r = pl.multiple_of(t * bt, bt)
            n_re = a_re * h_re - a_im * h_im + xr_scr[pl.ds(r, bt), cols]
            n_im = a_re * h_im + a_im * h_re + xi_scr[pl.ds(r, bt), cols]
            xr_scr[pl.ds(r, bt), cols] = n_re
            xi_scr[pl.ds(r, bt), cols] = n_im
            return n_re, n_im

        h_re, h_im = lax.fori_loop(0, l, step, (hr_ref[:, cols], hi_ref[:, cols]))
        hr_ref[:, cols] = h_re
        hi_ref[:, cols] = h_im

    y_parts = []
    for half in range(2):
        cols = slice(half * SSM_HALF_STATE, (half + 1) * SSM_HALF_STATE)
        y_parts.append(mm.dot(xr_scr[:, cols], ccre_ref[half]) - mm.dot(xi_scr[:, cols], ccim_ref[half]))
    y = jnp.concatenate(y_parts, axis=-1) + d_ref[...] * u_tb
    gy = 0.5 * y * (1.0 + jnp.tanh(0.7978845608028654 * (y + 0.044715 * (y * y * y))))
    ob = gy * _sigmoid(mm.dot(gy, wglu_ref[...]) + bglu_ref[...])
    ob = _rms(ob, nssm_ref[...])
    out_ref[:, :, W_A:] = pltpu.einshape("lbc->blc", ob.reshape(l, bt, W_B)).astype(out_ref.dtype)

    if lookahead:
        proj_scr[pl.ds(pl.multiple_of((1 - slot) * rows, rows), rows), :] = project(xnext_ref)


def _mixer(x, s0, c0, hr0, hi0, p, l, mm):
    b, t, _ = x.shape
    bt = MIX_BT
    rows = bt * l
    nchunks = t // l
    grid = (b // bt, nchunks)

    def state(shape, **kw):
        nd = len(shape)
        return pl.BlockSpec((bt,) + shape, lambda i, n: (i,) + (0,) * nd, **kw)

    def const(shape):
        return _resident(shape, 2)

    x_first = pl.BlockSpec((bt, l, D_MODEL), lambda i, n: (i, 0, 0), pipeline_mode=pl.Buffered(1))
    x_next = pl.BlockSpec((bt, l, D_MODEL), lambda i, n: (i, jnp.minimum(n + 1, nchunks - 1), 0))
    consts = [p["norm_mix"], p["w_in"]]
    consts_tail = [p["conv_w"], p["a_log_row"], p["dt_bias_row"], p["norm_gate"],
              p["a_re"], p["a_im"], p["bb_re"], p["bb_im"], p["cc_re"], p["cc_im"],
              p["ssm_d"], p["w_glu"], p["b_glu"], p["norm_ssm"]]
    state_shapes = [(H_A, DK, DV), (K_CONV - 1, QKV_DIM), (SSM_STATE,), (SSM_STATE,)]
    out_shapes = ([jax.ShapeDtypeStruct((b, t, D_MODEL), mm.act)]
                  + [jax.ShapeDtypeStruct((b,) + s, F32) for s in state_shapes])
    return pl.pallas_call(
        functools.partial(_mixer_kernel, bt=bt, l=l, lookahead=nchunks > 1, mm=mm),
        grid=grid,
        in_specs=([x_first, x_next] + [const(c.shape) for c in consts]
                  + [state(s, pipeline_mode=pl.Buffered(1)) for s in state_shapes]
                  + [const(c.shape) for c in consts_tail]),
        out_specs=[pl.BlockSpec((bt, l, D_MODEL), lambda i, n: (i, n, 0))] + [state(s) for s in state_shapes],
        out_shape=out_shapes,
        scratch_shapes=[pltpu.VMEM((2 * rows, PROJ_PAD), F32),
                        pltpu.VMEM((bt, CONV_PAD + l, QKV_DIM), F32),
                        pltpu.VMEM((rows, SSM_STATE), F32),
                        pltpu.VMEM((rows, SSM_STATE), F32)],
        compiler_params=pltpu.CompilerParams(dimension_semantics=("parallel", "arbitrary"),
                                             vmem_limit_bytes=VMEM_LIMIT),
        name="mixer",
    )(x, x, *consts, s0, c0, hr0, hi0, *consts_tail)


def _out_ffn_kernel(mix_ref, x_ref, wo_ref, g_ref, wg_ref, wu_ref, wd_ref, o_ref, *, mm):
    x1 = x_ref[...] + mm.dot(mix_ref[...], wo_ref[...])
    hn = _rms(x1, g_ref[...]).astype(mm.act)
    acc = x1
    for c in range(D_FF // D_FF_EXPERT):
        cols = slice(c * D_FF_EXPERT, (c + 1) * D_FF_EXPERT)
        hidden = _silu(mm.dot(hn, wg_ref[:, cols])) * mm.dot(hn, wu_ref[:, cols])
        acc = acc + mm.dot(hidden, wd_ref[cols, :])
    o_ref[...] = acc


def _out_ffn(mixed, x, w_out, gain, wg, wu, wd, tm, mm):
    n = x.shape[0]
    row = lambda w: pl.BlockSpec((tm, w), lambda i: (i, 0))
    const = lambda a: _resident(a.shape, 1)
    return pl.pallas_call(
        functools.partial(_out_ffn_kernel, mm=mm),
        grid=(pl.cdiv(n, tm),),
        in_specs=[row(D_MODEL), row(D_MODEL), const(w_out), const(gain), const(wg), const(wu), const(wd)],
        out_specs=row(D_MODEL),
        out_shape=jax.ShapeDtypeStruct((n, D_MODEL), F32),
        compiler_params=pltpu.CompilerParams(dimension_semantics=("parallel",),
                                             vmem_limit_bytes=VMEM_LIMIT),
        name="out_ffn",
    )(mixed, x, w_out, gain, wg, wu, wd)


def _out_router_kernel(mix_ref, x_ref, wo_ref, g_ref, wr_ref, x1_ref, hn_ref, comb_ref, slot_ref, slott_ref,
                       *, mm):
    x1 = x_ref[...] + mm.dot(mix_ref[...], wo_ref[...])
    x1_ref[...] = x1
    hn = _rms(x1, g_ref[...]).astype(mm.act)
    hn_ref[...] = hn
    logits = mm.dot(hn, wr_ref[...])
    lane = lax.broadcasted_iota(jnp.int32, logits.shape, 1).astype(F32)
    neg = jnp.float32(-jnp.inf)
    logits = jnp.where(lane < N_EXPERTS, logits, neg)
    v1 = jnp.max(logits, axis=-1, keepdims=True)
    i1 = jnp.min(jnp.where(logits == v1, lane, float(LANES)), axis=-1, keepdims=True)
    rest = jnp.where(lane == i1, neg, logits)
    v2 = jnp.max(rest, axis=-1, keepdims=True)
    i2 = jnp.min(jnp.where(rest == v2, lane, float(LANES)), axis=-1, keepdims=True)
    e2 = jnp.exp(v2 - v1)
    den = 1.0 + e2
    comb = jnp.where(lane == i1, 1.0 / den, jnp.where(lane == i2, e2 / den, 0.0))
    comb_ref[...] = comb
    routed = comb > 0.0
    routed_t = routed.astype(F32).T[:N_EXPERTS, :] > 0.0
    r_i = lax.broadcasted_iota(jnp.int32, (MOE_SUB, MOE_SUB), 0)
    c_i = lax.broadcasted_iota(jnp.int32, (MOE_SUB, MOE_SUB), 1)
    earlier_rows = jnp.where(c_i < r_i, 1.0, 0.0).astype(BF16)
    earlier_cols = jnp.where(r_i < c_i, 1.0, 0.0).astype(BF16)
    for s in range(comb.shape[0] // MOE_SUB):
        rows = slice(s * MOE_SUB, (s + 1) * MOE_SUB)
        rank = jnp.dot(earlier_rows, jnp.where(routed[rows], 1.0, 0.0).astype(BF16),
                       preferred_element_type=F32)
        slot_ref[rows, :] = jnp.where(routed[rows], rank, -1.0)
        rank_t = jnp.dot(jnp.where(routed_t[:, rows], 1.0, 0.0).astype(BF16), earlier_cols,
                         preferred_element_type=F32)
        slott_ref[:, rows] = jnp.where(routed_t[:, rows], rank_t, -1.0)


def _out_router(mixed, x, w_out, gain, w_router, tm, mm):
    n = x.shape[0]
    row = lambda w: pl.BlockSpec((tm, w), lambda i: (i, 0))
    const = lambda a: _resident(a.shape, 1)
    return pl.pallas_call(
        functools.partial(_out_router_kernel, mm=mm),
        grid=(pl.cdiv(n, tm),),
        in_specs=[row(D_MODEL), row(D_MODEL), const(w_out), const(gain), const(w_router)],
        out_specs=[row(D_MODEL), row(D_MODEL), row(LANES), row(LANES),
                   pl.BlockSpec((N_EXPERTS, tm), lambda i: (0, i))],
        out_shape=[jax.ShapeDtypeStruct((n, D_MODEL), F32),
                   jax.ShapeDtypeStruct((n, D_MODEL), mm.act),
                   jax.ShapeDtypeStruct((n, LANES), F32),
                   jax.ShapeDtypeStruct((n, LANES), F32),
                   jax.ShapeDtypeStruct((N_EXPERTS, n), F32)],
        compiler_params=pltpu.CompilerParams(dimension_semantics=("parallel",),
                                             vmem_limit_bytes=VMEM_LIMIT),
        name="out_router",
    )(mixed, x, w_out, gain, w_router)


def _moe_kernel(cnt_ref, hn_ref, x1_ref, comb_ref, slot_ref, slott_ref, wg_ref, wu_ref, wd_ref, gf_ref, o_ref,
                acc_ref, xc_ref, y_ref, *, tm, mm):
    i = pl.program_id(0)
    e = pl.program_id(1)
    nsub = tm // MOE_SUB

    @pl.when((i == 0) & (e == 0))
    def _():
        xc_ref[...] = jnp.zeros_like(xc_ref)
        y_ref[...] = jnp.zeros_like(y_ref)

    @pl.when(e == 0)
    def _():
        acc_ref[...] = jnp.zeros_like(acc_ref)

    lane = lax.broadcasted_iota(jnp.int32, (MOE_SUB, LANES), 1)
    expert_row = lax.broadcasted_iota(jnp.int32, (N_EXPERTS, MOE_SUB), 0) == e

    spans = []
    base = 0
    for s in range(nsub):
        cnt = cnt_ref[(i * nsub + s) * N_EXPERTS + e]
        spans.append(base)
        base = base + ((cnt + MOE_ALIGN - 1) // MOE_ALIGN) * MOE_ALIGN
    total = base

    slot_row = lax.broadcasted_iota(jnp.int32, (MOE_SUB, MOE_SUB), 0).astype(F32)
    for s, base_s in enumerate(spans):
        rows = slice(s * MOE_SUB, (s + 1) * MOE_SUB)
        slot = jnp.sum(jnp.where(expert_row, slott_ref[:, rows], 0.0), axis=0, keepdims=True)
        onehot = jnp.where(slot == slot_row, 1.0, 0.0).astype(BF16)
        dst = pl.ds(pl.multiple_of(base_s, MOE_ALIGN), MOE_SUB)
        xc_ref[dst, :] = mm.dot(onehot, hn_ref[rows, :])

    def ffn(c, carry):
        r = pl.ds(pl.multiple_of(c * MOE_CHUNK, MOE_CHUNK), MOE_CHUNK)
        x = xc_ref[r, :].astype(mm.act)
        hidden = _silu(mm.dot(x, wg_ref[0])) * mm.dot(x, wu_ref[0])
        y_ref[r, :] = mm.dot(hidden, wd_ref[0])
        return carry

    lax.fori_loop(0, (total + MOE_CHUNK - 1) // MOE_CHUNK, ffn, 0)

    pieces = 1 if mm.precise else 2
    slot_lane = lax.broadcasted_iota(jnp.int32, (MOE_SUB, pieces * MOE_SUB), 1)
    slot_lane = jnp.where(slot_lane >= MOE_SUB, slot_lane - MOE_SUB, slot_lane).astype(F32)
    for s, base_s in enumerate(spans):
        rows = slice(s * MOE_SUB, (s + 1) * MOE_SUB)
        gate = jnp.sum(jnp.where(lane == e, comb_ref[rows, :], 0.0), axis=-1, keepdims=True)
        slot = jnp.sum(jnp.where(lane == e, slot_ref[rows, :], 0.0), axis=-1, keepdims=True)
        y = y_ref[pl.ds(pl.multiple_of(base_s, MOE_ALIGN), MOE_SUB), :]
        if not mm.precise:
            y_hi = y.astype(BF16)
            y = jnp.concatenate([y_hi, (y - y_hi.astype(F32)).astype(BF16)], axis=0)
        onehot = jnp.where(slot == slot_lane, 1.0, 0.0)
        back = mm.dot(onehot, y)
        acc_ref[rows, :] += gate * back

    @pl.when(e == N_EXPERTS - 1)
    def _():
        o_ref[...] = _rms(x1_ref[...] + acc_ref[...], gf_ref[...])


def _moe(hn, x1, comb, slot, slot_t, wg, wu, wd, gain_final, tm, mm):
    n = x1.shape[0]
    assert n % tm == 0 and tm % MOE_SUB == 0
    routed = comb[:, :N_EXPERTS] > 0.0
    counts = jnp.sum(routed.reshape(n // MOE_SUB, MOE_SUB, N_EXPERTS), axis=1, dtype=jnp.int32).reshape(-1)
    row = lambda w: pl.BlockSpec((tm, w), lambda i, e, cnt: (i, 0))
    expert = lambda a, b: pl.BlockSpec((1, a, b), lambda i, e, cnt: (e, 0, 0))
    packed_rows = tm + max(MOE_CHUNK, MOE_SUB) + (tm // MOE_SUB) * MOE_ALIGN
    return pl.pallas_call(
        functools.partial(_moe_kernel, tm=tm, mm=mm),
        grid_spec=pltpu.PrefetchScalarGridSpec(
            num_scalar_prefetch=1,
            grid=(n // tm, N_EXPERTS),
            in_specs=[row(D_MODEL), row(D_MODEL), row(LANES), row(LANES),
                      pl.BlockSpec((N_EXPERTS, tm), lambda i, e, cnt: (0, i)),
                      expert(D_MODEL, D_FF_EXPERT), expert(D_MODEL, D_FF_EXPERT), expert(D_FF_EXPERT, D_MODEL),
                      pl.BlockSpec((1, D_MODEL), lambda i, e, cnt: (0, 0))],
            out_specs=row(D_MODEL),
            scratch_shapes=[pltpu.VMEM((tm, D_MODEL), F32),
                            pltpu.VMEM((packed_rows, D_MODEL), F32),
                            pltpu.VMEM((packed_rows, D_MODEL), F32)]),
        out_shape=jax.ShapeDtypeStruct((n, D_MODEL), F32),
        compiler_params=pltpu.CompilerParams(dimension_semantics=("arbitrary", "arbitrary"),
                                             vmem_limit_bytes=VMEM_LIMIT),
        name="moe",
    )(counts, hn, x1, comb, slot, slot_t, wg, wu, wd, gain_final)


def _lane_row(vec, offset):
    return jnp.zeros((1, LANES), F32).at[0, offset:offset + vec.shape[0]].set(vec.astype(F32))


def _block_diag_halves(w):
    g, a, b = w.shape
    hg = g // 2
    w = w.reshape(2, hg, a, b)
    eye = jnp.eye(hg, dtype=w.dtype)
    return jnp.einsum("sgab,gh->sgahb", w, eye).reshape(2, hg * a, hg * b)


MATMUL_WEIGHTS = ("w_in", "bb_re", "bb_im", "cc_re", "cc_im", "w_glu", "w_out")


def _layer_params(l, a):
    w_in = a["w_in"][l]
    w_cat = jnp.concatenate([w_in[:, :QKV_DIM + W_A], w_in[:, QKV_DIM + W_A + 2 * H_A:],
                             w_in[:, QKV_DIM + W_A:QKV_DIM + W_A + 2 * H_A],
                             jnp.zeros((D_MODEL, LANES - 2 * H_A), F32)], axis=1)
    lam_re = a["ssm_a_re"][l]
    lam_im = a["ssm_a_im"][l]
    delta = jnp.exp(a["ssm_log_dt"][l])[:, None]
    mag = jnp.exp(lam_re * delta)
    ab_re = mag * jnp.cos(lam_im * delta)
    ab_im = mag * jnp.sin(lam_im * delta)
    den = lam_re * lam_re + lam_im * lam_im
    f_re = ((ab_re - 1.0) * lam_re + ab_im * lam_im) / den
    f_im = (ab_im * lam_re - (ab_re - 1.0) * lam_im) / den
    b_re = a["ssm_b_re"][l]
    b_im = a["ssm_b_im"][l]
    bb_re = f_re[..., None] * b_re - f_im[..., None] * b_im
    bb_im = f_re[..., None] * b_im + f_im[..., None] * b_re
    return dict(
        norm_mix=a["norm_mix"][l][None], w_in=w_cat,
        conv_w=a["conv_w"][l],
        a_log_row=_lane_row(a["a_log"][l], H_A), dt_bias_row=_lane_row(a["dt_bias"][l], H_A),
        norm_gate=a["norm_gate"][l][None],
        a_re=ab_re.reshape(1, SSM_STATE), a_im=ab_im.reshape(1, SSM_STATE),
        bb_re=_block_diag_halves(jnp.swapaxes(bb_re, 1, 2)),
        bb_im=_block_diag_halves(jnp.swapaxes(bb_im, 1, 2)),
        cc_re=_block_diag_halves(jnp.swapaxes(a["ssm_c_re"][l], 1, 2)),
        cc_im=_block_diag_halves(jnp.swapaxes(a["ssm_c_im"][l], 1, 2)),
        ssm_d=a["ssm_d"][l][None], w_glu=a["w_glu"][l], b_glu=a["b_glu"][l][None],
        norm_ssm=a["norm_ssm"][l][None],
        w_out=a["w_out"][l], norm_ffn=a["norm_ffn"][l][None])


def _trunk(x, states, layers, ffn, moe, l, tm, mm):
    b, t, _ = x.shape
    n = b * t
    xf = x.reshape(n, D_MODEL)
    new_states = []
    for li, p in enumerate(layers):
        s0, c0, hr0, hi0 = states[li]
        mixed, s1, c1, hr1, hi1 = _mixer(xf.reshape(b, t, D_MODEL), s0, c0, hr0.reshape(b, SSM_STATE),
                                         hi0.reshape(b, SSM_STATE), p, l, mm)
        new_states.append((s1, c1, hr1.reshape(b, G_B, P_STATE), hi1.reshape(b, G_B, P_STATE)))
        mixed = mixed.reshape(n, D_MODEL)
        if li == 0:
            xf = _out_ffn(mixed, xf, p["w_out"], p["norm_ffn"], ffn["wg"], ffn["wu"], ffn["wd"], tm, mm)
        else:
            x1, hn, comb, slot, slot_t = _out_router(mixed, xf, p["w_out"], p["norm_ffn"], ffn["router"], tm, mm)
            xf = _moe(hn, x1, comb, slot, slot_t, moe["wg"], moe["wu"], moe["wd"], moe["norm_final"],
                      min(MOE_TM, n), _Matmul(precise=False))
    return xf.reshape(b, t, D_MODEL), new_states


def kernel(x_prompt, x_sample, state_delta, state_conv, state_ssm_re, state_ssm_im, meta_tokens, norm_mix, w_in, conv_w, a_log, dt_bias, norm_gate, ssm_a_re, ssm_a_im, ssm_b_re, ssm_b_im, ssm_c_re, ssm_c_im, ssm_d, ssm_log_dt, w_glu, b_glu, norm_ssm, w_out, norm_ffn, ffn_w_gate, ffn_w_up, ffn_w_down, router_w, moe_w_gate, moe_w_up, moe_w_down, norm_final):
    a = dict(norm_mix=norm_mix, w_in=w_in, conv_w=conv_w, a_log=a_log, dt_bias=dt_bias, norm_gate=norm_gate,
             ssm_a_re=ssm_a_re, ssm_a_im=ssm_a_im, ssm_b_re=ssm_b_re, ssm_b_im=ssm_b_im,
             ssm_c_re=ssm_c_re, ssm_c_im=ssm_c_im, ssm_d=ssm_d, ssm_log_dt=ssm_log_dt,
             w_glu=w_glu, b_glu=b_glu, norm_ssm=norm_ssm, w_out=w_out, norm_ffn=norm_ffn)
    depth = w_in.shape[0]

    layers32 = [_layer_params(l, a) for l in range(depth)]
    ffn32 = dict(wg=ffn_w_gate[0], wu=ffn_w_up[0], wd=ffn_w_down[0],
                 router=jnp.pad(router_w[0], ((0, 0), (0, LANES - N_EXPERTS))))
    moe = dict(wg=moe_w_gate[0].astype(BF16), wu=moe_w_up[0].astype(BF16), wd=moe_w_down[0].astype(BF16),
               norm_final=norm_final[None])

    def weights(mm):
        layers = [{k: (v.astype(mm.act) if k in MATMUL_WEIGHTS else v) for k, v in p.items()} for p in layers32]
        return layers, {k: v.astype(mm.act) for k, v in ffn32.items()}, moe

    bp = x_prompt.shape[0]
    bs, ts = x_sample.shape[0], x_sample.shape[1]
    side_b = 2 * MIX_BT
    pad = side_b - bs - 1
    x_side = jnp.concatenate([x_sample, meta_tokens[None], jnp.zeros((pad, ts, D_MODEL), F32)], axis=0)

    def side_state(st):
        zeros = jnp.zeros((side_b - bs,) + st.shape[1:], F32)
        return jnp.concatenate([st, zeros], axis=0)

    side_states = [(side_state(state_delta[l]), side_state(state_conv[l]),
                    side_state(state_ssm_re[l]), side_state(state_ssm_im[l])) for l in range(depth)]
    precise = _Matmul(precise=True)
    y_side, side_new = _trunk(x_side, side_states, *weights(precise), ts, side_b * ts, precise)

    def from_meta(st):
        return jnp.broadcast_to(st[bs:bs + 1], (bp,) + st.shape[1:])

    main_states = [tuple(from_meta(st) for st in side_new[l]) for l in range(depth)]
    fast = _Matmul(precise=False)
    y_prompt, main_new = _trunk(x_prompt, main_states, *weights(fast), CHUNK, 512, fast)

    def stack(new, idx, count):
        return jnp.stack([new[l][idx][:count] for l in range(depth)])

    return (y_prompt, y_side[:bs],
            stack(main_new, 0, bp), stack(main_new, 1, bp), stack(main_new, 2, bp), stack(main_new, 3, bp),
            stack(side_new, 0, bs), stack(side_new, 1, bs), stack(side_new, 2, bs), stack(side_new, 3, bs))
```

```python
import functools

import jax
import jax.numpy as jnp
from jax import lax
from jax.experimental import pallas as pl
from jax.experimental.pallas import tpu as pltpu

F32 = jnp.float32
BF16 = jnp.bfloat16

D_MODEL = 1024
N_META = 16
H_A = 4
DK = 128
DV = 128
W_A = H_A * DV
K_CONV = 4
QKV_DIM = 2 * H_A * DK + H_A * DV
W_B = D_MODEL - W_A
GROUP_CH = 16
G_B = W_B // GROUP_CH
P_STATE = 64
SSM_STATE = G_B * P_STATE
D_FF = 2816
N_EXPERTS = 8
D_FF_EXPERT = 1408
EPS = 1e-6

LANES = 128
SUBLANES = 8
PROJ_PAD = QKV_DIM + W_A + W_B + LANES
COL_Z = QKV_DIM
COL_U = QKV_DIM + W_A
COL_BA = QKV_DIM + W_A + W_B
CONV_PAD = SUBLANES
SSM_IN_BLOCKS = W_B // LANES
SSM_HALF_STATE = SSM_STATE // 2
MIX_BT = 8
CHUNK = 64
MOE_TM = 1024
MOE_SUB = LANES
MOE_CHUNK = 128
MOE_ALIGN = SUBLANES
VMEM_LIMIT = 60 * 1024 * 1024


class _Matmul:
    def __init__(self, precise):
        self.precise = precise
        self.act = F32 if precise else BF16

    def _args(self, a, b):
        if self.precise:
            return (a.astype(F32), b.astype(F32)), dict(precision=lax.Precision.HIGHEST)
        return (a.astype(BF16), b.astype(BF16)), {}

    def dot(self, a, b):
        args, kw = self._args(a, b)
        return jnp.dot(*args, preferred_element_type=F32, **kw)

    def einsum(self, spec, a, b):
        args, kw = self._args(a, b)
        return jnp.einsum(spec, *args, preferred_element_type=F32, **kw)


def _sigmoid(x):
    return 1.0 / (1.0 + jnp.exp(-x))


def _silu(x):
    return x * _sigmoid(x)


def _rms(x, gain):
    return x * lax.rsqrt(jnp.mean(x * x, axis=-1, keepdims=True) + EPS) * gain


def _resident(shape, grid_rank):
    zeros = (0,) * len(shape)
    index_map = (lambda i: zeros) if grid_rank == 1 else (lambda i, j: zeros)
    return pl.BlockSpec(shape, index_map, pipeline_mode=pl.Buffered(1))


def _delta_head(q, k, v, beta, gcum, s, mm):
    n, l, _ = q.shape
    p = n // 2

    def side2(x):
        return jnp.concatenate([x[:p], x[p:]], axis=-1)

    def diag2(x):
        zero = jnp.zeros_like(x[:p])
        return jnp.concatenate([jnp.concatenate([x[:p], zero], axis=-1),
                                jnp.concatenate([zero, x[p:]], axis=-1)], axis=1)

    def unside2(x2, w):
        return jnp.concatenate([x2[..., :w], x2[..., w:]], axis=0)

    first = lax.broadcasted_iota(jnp.int32, (p, l, 2 * l), 2) < l

    def per_row2(x):
        return jnp.where(first, x[:p], x[p:])

    def own_block2(x2):
        return jnp.concatenate([jnp.where(first, x2, 0.0), jnp.where(first, 0.0, x2)], axis=1)

    g_hi = gcum.astype(BF16).astype(F32)
    r1 = gcum - g_hi
    g_mid = r1.astype(BF16).astype(F32)
    g_lo = r1 - g_mid
    lane = lax.broadcasted_iota(jnp.int32, (n, l, LANES), 2)
    pieces = jnp.where(lane == 0, g_hi, jnp.where(lane == 1, g_mid, jnp.where(lane == 2, g_lo, 0.0)))
    ones = jnp.ones((p, l, 2 * LANES), F32)
    g_row = mm.einsum("bik,bjk->bij", ones, diag2(pieces))
    g_col = per_row2(gcum)
    ii = lax.broadcasted_iota(jnp.int32, (p, l, 2 * l), 1)
    jj = lax.broadcasted_iota(jnp.int32, (p, l, 2 * l), 2)
    jj = jnp.where(first, jj, jj - l)
    incl = ii >= jj
    strict = ii > jj
    decay = jnp.where(incl, jnp.exp(jnp.where(incl, g_col - g_row, 0.0)), 0.0)
    k_diag = diag2(k)
    kk = mm.einsum("bik,bjk->bij", side2(k), k_diag)
    qk = mm.einsum("bik,bjk->bij", side2(q), k_diag) * decay
    neg_m = jnp.where(strict, -(per_row2(beta) * kk * decay), 0.0)
    eye = jnp.where(ii == jj, 1.0, 0.0)
    t_inv = eye + neg_m
    pw = neg_m
    size = 2
    while size < l:
        pw = mm.einsum("bij,bjk->bik", pw, own_block2(pw))
        t_inv = t_inv + mm.einsum("bij,bjk->bik", t_inv, own_block2(pw))
        size *= 2
    eg = jnp.exp(gcum)
    rhs = jnp.concatenate([beta * v, (beta * eg) * k], axis=-1)
    sol = unside2(mm.einsum("bij,bjd->bid", t_inv, diag2(rhs)), 2 * DV)
    u0 = sol[..., :DV]
    w = sol[..., DV:]
    u = u0 - mm.einsum("bik,bkv->biv", w, s)
    o = mm.einsum("bik,bkv->biv", q * eg, s) + unside2(mm.einsum("bij,bjv->biv", qk, diag2(u)), DV)
    g_last = gcum[:, l - 1:l, :]
    kg = k * jnp.exp(g_last - gcum)
    s_new = s * jnp.exp(g_last) + mm.einsum("bjk,bjv->bkv", kg, u)
    return o, s_new


def _mixer_kernel(xfirst_ref, xnext_ref, gin_ref, win_ref, s0_ref, c0_ref, hr0_ref, hi0_ref,
                  convw_ref, alog_ref, dtb_ref, ngate_ref,
                  are_ref, aim_ref, bbre_ref, bbim_ref, ccre_ref, ccim_ref,
                  d_ref, wglu_ref, bglu_ref, nssm_ref,
                  out_ref, s_ref, c_ref, hr_ref, hi_ref,
                  proj_scr, xp_scr, xr_scr, xi_scr, *, bt, l, lookahead, mm):
    rows = bt * l
    chunk = pl.program_id(1)

    def project(x_ref):
        hn = _rms(x_ref[...].reshape(rows, D_MODEL), gin_ref[...])
        return mm.dot(hn, win_ref[...])

    @pl.when(chunk == 0)
    def _():
        s_ref[...] = s0_ref[...]
        hr_ref[...] = hr0_ref[...]
        hi_ref[...] = hi0_ref[...]
        xp_scr[:, CONV_PAD - (K_CONV - 1):CONV_PAD, :] = c0_ref[...]
        proj_scr[0:rows, :] = project(xfirst_ref)

    slot = chunk % 2
    cur = pl.ds(pl.multiple_of(slot * rows, rows), rows)

    def proj_cols(start, width):
        return proj_scr[cur, start:start + width].reshape(bt, l, width)

    xp_scr[:, CONV_PAD:CONV_PAD + l, :] = proj_cols(0, QKV_DIM)
    xp = xp_scr[...]
    acc = xp[:, CONV_PAD:, :] * convw_ref[K_CONV - 1:K_CONV, :]
    for back in range(1, K_CONV):
        shifted = pltpu.roll(xp, back, 1)[:, CONV_PAD:, :]
        acc = acc + shifted * convw_ref[K_CONV - 1 - back:K_CONV - back, :]
    tail = xp_scr[:, CONV_PAD + l - (K_CONV - 1):CONV_PAD + l, :]
    xp_scr[:, CONV_PAD - (K_CONV - 1):CONV_PAD, :] = tail
    c_ref[...] = tail
    qkv = _silu(acc)

    ba = proj_scr[cur, COL_BA:COL_BA + LANES]
    beta_all = _sigmoid(ba)
    sp_in = ba + dtb_ref[...]
    softplus = jnp.maximum(sp_in, 0.0) + jnp.log1p(jnp.exp(-jnp.abs(sp_in)))
    g_all = -jnp.exp(alog_ref[...]) * softplus
    t_idx = lax.broadcasted_iota(jnp.int32, (rows, LANES), 0) % l
    shift = 1
    while shift < l:
        g_all = g_all + jnp.where(t_idx >= shift, pltpu.roll(g_all, shift, 0), 0.0)
        shift *= 2
    beta_all = beta_all.reshape(bt, l, LANES)
    g_all = g_all.reshape(bt, l, LANES)

    def heads(first_col, width, src):
        return jnp.concatenate([src[:, :, first_col + h * width:first_col + (h + 1) * width]
                                for h in range(H_A)], axis=0)

    z = proj_cols(COL_Z, W_A)
    q = heads(0, DK, qkv)
    k = heads(H_A * DK, DK, qkv)
    v = heads(2 * H_A * DK, DV, qkv)
    q = q * lax.rsqrt(jnp.sum(q * q, axis=-1, keepdims=True) + EPS) * (DK ** -0.5)
    k = k * lax.rsqrt(jnp.sum(k * k, axis=-1, keepdims=True) + EPS)
    s_all = jnp.concatenate([s_ref[:, h] for h in range(H_A)], axis=0)
    o, s_new = _delta_head(q, k, v, heads(0, 1, beta_all), heads(H_A, 1, g_all), s_all, mm)
    o = _rms(o, ngate_ref[...]) * _silu(heads(0, DV, z))
    for h in range(H_A):
        s_ref[:, h] = s_new[h * bt:(h + 1) * bt]
        out_ref[:, :, h * DV:(h + 1) * DV] = o[h * bt:(h + 1) * bt].astype(out_ref.dtype)

    u_tb = pltpu.einshape("blc->lbc", proj_cols(COL_U, W_B)).reshape(rows, W_B)
    in_w = W_B // SSM_IN_BLOCKS
    st_w = SSM_STATE // SSM_IN_BLOCKS
    for blk in range(SSM_IN_BLOCKS):
        ub = u_tb[:, blk * in_w:(blk + 1) * in_w]
        cols = slice(blk * st_w, (blk + 1) * st_w)
        xr_scr[:, cols] = mm.dot(ub, bbre_ref[blk])
        xi_scr[:, cols] = mm.dot(ub, bbim_ref[blk])

    for half in range(2):
        cols = slice(half * SSM_HALF_STATE, (half + 1) * SSM_HALF_STATE)
        a_re = jnp.broadcast_to(are_ref[:, cols], (bt, SSM_HALF_STATE))
        a_im = jnp.broadcast_to(aim_ref[:, cols], (bt, SSM_HALF_STATE))

        h_re, h_im = hr_ref[:, cols], hi_ref[:, cols]
        for t in range(l):
            step_rows = slice(t * bt, (t + 1) * bt)
            h_re, h_im = (a_re * h_re - a_im * h_im + xr_scr[step_rows, cols],
                          a_re * h_im + a_im * h_re + xi_scr[step_rows, cols])
            xr_scr[step_rows, cols] = h_re
            xi_scr[step_rows, cols] = h_im
        hr_ref[:, cols] = h_re
        hi_ref[:, cols] = h_im

    y_parts = []
    for half in range(2):
        cols = slice(half * SSM_HALF_STATE, (half + 1) * SSM_HALF_STATE)
        y_parts.append(mm.dot(xr_scr[:, cols], ccre_ref[half]) - mm.dot(xi_scr[:, cols], ccim_ref[half]))
    y = jnp.concatenate(y_parts, axis=-1) + d_ref[...] * u_tb
    gy = 0.5 * y * (1.0 + jnp.tanh(0.7978845608028654 * (y + 0.044715 * (y * y * y))))
    ob = gy * _sigmoid(mm.dot(gy, wglu_ref[...]) + bglu_ref[...])
    ob = _rms(ob, nssm_ref[...])
    out_ref[:, :, W_A:] = pltpu.einshape("lbc->blc", ob.reshape(l, bt, W_B)).astype(out_ref.dtype)

    if lookahead:
        proj_scr[pl.ds(pl.multiple_of((1 - slot) * rows, rows), rows), :] = project(xnext_ref)


def _mixer(x, s0, c0, hr0, hi0, p, l, mm):
    b, t, _ = x.shape
    bt = MIX_BT
    rows = bt * l
    nchunks = t // l
    grid = (b // bt, nchunks)

    def state(shape, **kw):
        nd = len(shape)
        return pl.BlockSpec((bt,) + shape, lambda i, n: (i,) + (0,) * nd, **kw)

    def const(shape):
        return _resident(shape, 2)

    x_first = pl.BlockSpec((bt, l, D_MODEL), lambda i, n: (i, 0, 0), pipeline_mode=pl.Buffered(1))
    x_next = pl.BlockSpec((bt, l, D_MODEL), lambda i, n: (i, jnp.minimum(n + 1, nchunks - 1), 0))
    consts = [p["norm_mix"], p["w_in"]]
    consts_tail = [p["conv_w"], p["a_log_row"], p["dt_bias_row"], p["norm_gate"],
              p["a_re"], p["a_im"], p["bb_re"], p["bb_im"], p["cc_re"], p["cc_im"],
              p["ssm_d"], p["w_glu"], p["b_glu"], p["norm_ssm"]]
    state_shapes = [(H_A, DK, DV), (K_CONV - 1, QKV_DIM), (SSM_STATE,), (SSM_STATE,)]
    out_shapes = ([jax.ShapeDtypeStruct((b, t, D_MODEL), mm.act)]
                  + [jax.ShapeDtypeStruct((b,) + s, F32) for s in state_shapes])
    return pl.pallas_call(
        functools.partial(_mixer_kernel, bt=bt, l=l, lookahead=nchunks > 1, mm=mm),
        grid=grid,
        in_specs=([x_first, x_next] + [const(c.shape) for c in consts]
                  + [state(s, pipeline_mode=pl.Buffered(1)) for s in state_shapes]
                  + [const(c.shape) for c in consts_tail]),
        out_specs=[pl.BlockSpec((bt, l, D_MODEL), lambda i, n: (i, n, 0))] + [state(s) for s in state_shapes],
        out_shape=out_shapes,
        scratch_shapes=[pltpu.VMEM((2 * rows, PROJ_PAD), F32),
                        pltpu.VMEM((bt, CONV_PAD + l, QKV_DIM), F32),
                        pltpu.VMEM((rows, SSM_STATE), F32),
                        pltpu.VMEM((rows, SSM_STATE), F32)],
        compiler_params=pltpu.CompilerParams(dimension_semantics=("parallel", "arbitrary"),
                                             vmem_limit_bytes=VMEM_LIMIT),
        name="mixer",
    )(x, x, *consts, s0, c0, hr0, hi0, *consts_tail)


def _out_ffn_kernel(mix_ref, x_ref, wo_ref, g_ref, wg_ref, wu_ref, wd_ref, o_ref, *, mm):
    x1 = x_ref[...] + mm.dot(mix_ref[...], wo_ref[...])
    hn = _rms(x1, g_ref[...]).astype(mm.act)
    acc = x1
    for c in range(D_FF // D_FF_EXPERT):
        cols = slice(c * D_FF_EXPERT, (c + 1) * D_FF_EXPERT)
        hidden = _silu(mm.dot(hn, wg_ref[:, cols])) * mm.dot(hn, wu_ref[:, cols])
        acc = acc + mm.dot(hidden, wd_ref[cols, :])
    o_ref[...] = acc


def _out_ffn(mixed, x, w_out, gain, wg, wu, wd, tm, mm):
    n = x.shape[0]
    row = lambda w: pl.BlockSpec((tm, w), lambda i: (i, 0))
    const = lambda a: _resident(a.shape, 1)
    return pl.pallas_call(
        functools.partial(_out_ffn_kernel, mm=mm),
        grid=(pl.cdiv(n, tm),),
        in_specs=[row(D_MODEL), row(D_MODEL), const(w_out), const(gain), const(wg), const(wu), const(wd)],
        out_specs=row(D_MODEL),
        out_shape=jax.ShapeDtypeStruct((n, D_MODEL), F32),
        compiler_params=pltpu.CompilerParams(dimension_semantics=("parallel",),
                                             vmem_limit_bytes=VMEM_LIMIT),
        name="out_ffn",
    )(mixed, x, w_out, gain, wg, wu, wd)


def _out_router_kernel(mix_ref, x_ref, wo_ref, g_ref, wr_ref, x1_ref, hn_ref, comb_ref, slot_ref, slott_ref,
                       *, mm):
    x1 = x_ref[...] + mm.dot(mix_ref[...], wo_ref[...])
    x1_ref[...] = x1
    hn = _rms(x1, g_ref[...]).astype(mm.act)
    hn_ref[...] = hn
    logits = mm.dot(hn, wr_ref[...])
    lane = lax.broadcasted_iota(jnp.int32, logits.shape, 1).astype(F32)
    neg = jnp.float32(-jnp.inf)
    logits = jnp.where(lane < N_EXPERTS, logits, neg)
    v1 = jnp.max(logits, axis=-1, keepdims=True)
    i1 = jnp.min(jnp.where(logits == v1, lane, float(LANES)), axis=-1, keepdims=True)
    rest = jnp.where(lane == i1, neg, logits)
    v2 = jnp.max(rest, axis=-1, keepdims=True)
    i2 = jnp.min(jnp.where(rest == v2, lane, float(LANES)), axis=-1, keepdims=True)
    e2 = jnp.exp(v2 - v1)
    den = 1.0 + e2
    comb = jnp.where(lane == i1, 1.0 / den, jnp.where(lane == i2, e2 / den, 0.0))
    comb_ref[...] = comb
    routed = comb > 0.0
    routed_t = routed.astype(F32).T[:N_EXPERTS, :] > 0.0
    r_i = lax.broadcasted_iota(jnp.int32, (MOE_SUB, MOE_SUB), 0)
    c_i = lax.broadcasted_iota(jnp.int32, (MOE_SUB, MOE_SUB), 1)
    earlier_rows = jnp.where(c_i < r_i, 1.0, 0.0).astype(BF16)
    earlier_cols = jnp.where(r_i < c_i, 1.0, 0.0).astype(BF16)
    for s in range(comb.shape[0] // MOE_SUB):
        rows = slice(s * MOE_SUB, (s + 1) * MOE_SUB)
        rank = jnp.dot(earlier_rows, jnp.where(routed[rows], 1.0, 0.0).astype(BF16),
                       preferred_element_type=F32)
        slot_ref[rows, :] = jnp.where(routed[rows], rank, -1.0)
        rank_t = jnp.dot(jnp.where(routed_t[:, rows], 1.0, 0.0).astype(BF16), earlier_cols,
                         preferred_element_type=F32)
        slott_ref[:, rows] = jnp.where(routed_t[:, rows], rank_t, -1.0)


def _out_router(mixed, x, w_out, gain, w_router, tm, mm):
    n = x.shape[0]
    row = lambda w: pl.BlockSpec((tm, w), lambda i: (i, 0))
    const = lambda a: _resident(a.shape, 1)
    return pl.pallas_call(
        functools.partial(_out_router_kernel, mm=mm),
        grid=(pl.cdiv(n, tm),),
        in_specs=[row(D_MODEL), row(D_MODEL), const(w_out), const(gain), const(w_router)],
        out_specs=[row(D_MODEL), row(D_MODEL), row(LANES), row(LANES),
                   pl.BlockSpec((N_EXPERTS, tm), lambda i: (0, i))],
        out_shape=[jax.ShapeDtypeStruct((n, D_MODEL), F32),
                   jax.ShapeDtypeStruct((n, D_MODEL), mm.act),
                   jax.ShapeDtypeStruct((n, LANES), F32),
                   jax.ShapeDtypeStruct((n, LANES), F32),
                   jax.ShapeDtypeStruct((N_EXPERTS, n), F32)],
        compiler_params=pltpu.CompilerParams(dimension_semantics=("parallel",),
                                             vmem_limit_bytes=VMEM_LIMIT),
        name="out_router",
    )(mixed, x, w_out, gain, w_router)


def _moe_kernel(cnt_ref, hn_ref, x1_ref, comb_ref, slot_ref, slott_ref, wg_ref, wu_ref, wd_ref, gf_ref, o_ref,
                acc_ref, xc_ref, y_ref, *, tm, mm):
    i = pl.program_id(0)
    e = pl.program_id(1)
    nsub = tm // MOE_SUB

    @pl.when((i == 0) & (e == 0))
    def _():
        xc_ref[...] = jnp.zeros_like(xc_ref)
        y_ref[...] = jnp.zeros_like(y_ref)

    @pl.when(e == 0)
    def _():
        acc_ref[...] = jnp.zeros_like(acc_ref)

    lane = lax.broadcasted_iota(jnp.int32, (MOE_SUB, LANES), 1)
    expert_row = lax.broadcasted_iota(jnp.int32, (N_EXPERTS, MOE_SUB), 0) == e

    spans = []
    base = 0
    for s in range(nsub):
        cnt = cnt_ref[(i * nsub + s) * N_EXPERTS + e]
        spans.append(base)
        base = base + ((cnt + MOE_ALIGN - 1) // MOE_ALIGN) * MOE_ALIGN
    total = base

    slot_row = lax.broadcasted_iota(jnp.int32, (MOE_SUB, MOE_SUB), 0).astype(F32)
    for s, base_s in enumerate(spans):
        rows = slice(s * MOE_SUB, (s + 1) * MOE_SUB)
        slot = jnp.sum(jnp.where(expert_row, slott_ref[:, rows], 0.0), axis=0, keepdims=True)
        onehot = jnp.where(slot == slot_row, 1.0, 0.0).astype(BF16)
        dst = pl.ds(pl.multiple_of(base_s, MOE_ALIGN), MOE_SUB)
        xc_ref[dst, :] = mm.dot(onehot, hn_ref[rows, :])

    def ffn(c, carry):
        r = pl.ds(pl.multiple_of(c * MOE_CHUNK, MOE_CHUNK), MOE_CHUNK)
        x = xc_ref[r, :].astype(mm.act)
        hidden = _silu(mm.dot(x, wg_ref[0])) * mm.dot(x, wu_ref[0])
        y_ref[r, :] = mm.dot(hidden, wd_ref[0])
        return carry

    lax.fori_loop(0, (total + MOE_CHUNK - 1) // MOE_CHUNK, ffn, 0)

    pieces = 1 if mm.precise else 2
    slot_lane = lax.broadcasted_iota(jnp.int32, (MOE_SUB, pieces * MOE_SUB), 1)
    slot_lane = jnp.where(slot_lane >= MOE_SUB, slot_lane - MOE_SUB, slot_lane).astype(F32)
    for s, base_s in enumerate(spans):
        rows = slice(s * MOE_SUB, (s + 1) * MOE_SUB)
        gate = jnp.sum(jnp.where(lane == e, comb_ref[rows, :], 0.0), axis=-1, keepdims=True)
        slot = jnp.sum(jnp.where(lane == e, slot_ref[rows, :], 0.0), axis=-1, keepdims=True)
        y = y_ref[pl.ds(pl.multiple_of(base_s, MOE_ALIGN), MOE_SUB), :]
        if not mm.precise:
            y_hi = y.astype(BF16)
            y = jnp.concatenate([y_hi, (y - y_hi.astype(F32)).astype(BF16)], axis=0)
        onehot = jnp.where(slot == slot_lane, 1.0, 0.0)
        back = mm.dot(onehot, y)
        acc_ref[rows, :] += gate * back

    @pl.when(e == N_EXPERTS - 1)
    def _():
        o_ref[...] = _rms(x1_ref[...] + acc_ref[...], gf_ref[...])


def _moe(hn, x1, comb, slot, slot_t, wg, wu, wd, gain_final, tm, mm):
    n = x1.shape[0]
    assert n % tm == 0 and tm % MOE_SUB == 0
    routed = comb[:, :N_EXPERTS] > 0.0
    counts = jnp.sum(routed.reshape(n // MOE_SUB, MOE_SUB, N_EXPERTS), axis=1, dtype=jnp.int32).reshape(-1)
    row = lambda w: pl.BlockSpec((tm, w), lambda i, e, cnt: (i, 0))
    expert = lambda a, b: pl.BlockSpec((1, a, b), lambda i, e, cnt: (e, 0, 0))
    packed_rows = tm + max(MOE_CHUNK, MOE_SUB) + (tm // MOE_SUB) * MOE_ALIGN
    return pl.pallas_call(
        functools.partial(_moe_kernel, tm=tm, mm=mm),
        grid_spec=pltpu.PrefetchScalarGridSpec(
            num_scalar_prefetch=1,
            grid=(n // tm, N_EXPERTS),
            in_specs=[row(D_MODEL), row(D_MODEL), row(LANES), row(LANES),
                      pl.BlockSpec((N_EXPERTS, tm), lambda i, e, cnt: (0, i)),
                      expert(D_MODEL, D_FF_EXPERT), expert(D_MODEL, D_FF_EXPERT), expert(D_FF_EXPERT, D_MODEL),
                      pl.BlockSpec((1, D_MODEL), lambda i, e, cnt: (0, 0))],
            out_specs=row(D_MODEL),
            scratch_shapes=[pltpu.VMEM((tm, D_MODEL), F32),
                            pltpu.VMEM((packed_rows, D_MODEL), F32),
                            pltpu.VMEM((packed_rows, D_MODEL), F32)]),
        out_shape=jax.ShapeDtypeStruct((n, D_MODEL), F32),
        compiler_params=pltpu.CompilerParams(dimension_semantics=("arbitrary", "arbitrary"),
                                             vmem_limit_bytes=VMEM_LIMIT),
        name="moe",
    )(counts, hn, x1, comb, slot, slot_t, wg, wu, wd, gain_final)


def _lane_row(vec, offset):
    return jnp.zeros((1, LANES), F32).at[0, offset:offset + vec.shape[0]].set(vec.astype(F32))


def _block_diag(w, nblk):
    g, a, b = w.shape
    hg = g // nblk
    w = w.reshape(nblk, hg, a, b)
    eye = jnp.eye(hg, dtype=w.dtype)
    return jnp.einsum("sgab,gh->sgahb", w, eye).reshape(nblk, hg * a, hg * b)


MATMUL_WEIGHTS = ("w_in", "bb_re", "bb_im", "cc_re", "cc_im", "w_glu", "w_out")


def _layer_params(l, a):
    w_in = a["w_in"][l]
    w_cat = jnp.concatenate([w_in[:, :QKV_DIM + W_A], w_in[:, QKV_DIM + W_A + 2 * H_A:],
                             w_in[:, QKV_DIM + W_A:QKV_DIM + W_A + 2 * H_A],
                             jnp.zeros((D_MODEL, LANES - 2 * H_A), F32)], axis=1)
    lam_re = a["ssm_a_re"][l]
    lam_im = a["ssm_a_im"][l]
    delta = jnp.exp(a["ssm_log_dt"][l])[:, None]
    mag = jnp.exp(lam_re * delta)
    ab_re = mag * jnp.cos(lam_im * delta)
    ab_im = mag * jnp.sin(lam_im * delta)
    den = lam_re * lam_re + lam_im * lam_im
    f_re = ((ab_re - 1.0) * lam_re + ab_im * lam_im) / den
    f_im = (ab_im * lam_re - (ab_re - 1.0) * lam_im) / den
    b_re = a["ssm_b_re"][l]
    b_im = a["ssm_b_im"][l]
    bb_re = f_re[..., None] * b_re - f_im[..., None] * b_im
    bb_im = f_re[..., None] * b_im + f_im[..., None] * b_re
    return dict(
        norm_mix=a["norm_mix"][l][None], w_in=w_cat,
        conv_w=a["conv_w"][l],
        a_log_row=_lane_row(a["a_log"][l], H_A), dt_bias_row=_lane_row(a["dt_bias"][l], H_A),
        norm_gate=a["norm_gate"][l][None],
        a_re=ab_re.reshape(1, SSM_STATE), a_im=ab_im.reshape(1, SSM_STATE),
        bb_re=_block_diag(jnp.swapaxes(bb_re, 1, 2), SSM_IN_BLOCKS),
        bb_im=_block_diag(jnp.swapaxes(bb_im, 1, 2), SSM_IN_BLOCKS),
        cc_re=_block_diag(jnp.swapaxes(a["ssm_c_re"][l], 1, 2), 2),
        cc_im=_block_diag(jnp.swapaxes(a["ssm_c_im"][l], 1, 2), 2),
        ssm_d=a["ssm_d"][l][None], w_glu=a["w_glu"][l], b_glu=a["b_glu"][l][None],
        norm_ssm=a["norm_ssm"][l][None],
        w_out=a["w_out"][l], norm_ffn=a["norm_ffn"][l][None])


def _trunk(x, states, layers, ffn, moe, l, tm, mm):
    b, t, _ = x.shape
    n = b * t
    xf = x.reshape(n, D_MODEL)
    new_states = []
    for li, p in enumerate(layers):
        s0, c0, hr0, hi0 = states[li]
        mixed, s1, c1, hr1, hi1 = _mixer(xf.reshape(b, t, D_MODEL), s0, c0, hr0.reshape(b, SSM_STATE),
                                         hi0.reshape(b, SSM_STATE), p, l, mm)
        new_states.append((s1, c1, hr1.reshape(b, G_B, P_STATE), hi1.reshape(b, G_B, P_STATE)))
        mixed = mixed.reshape(n, D_MODEL)
        if li == 0:
            xf = _out_ffn(mixed, xf, p["w_out"], p["norm_ffn"], ffn["wg"], ffn["wu"], ffn["wd"], tm, mm)
        else:
            x1, hn, comb, slot, slot_t = _out_router(mixed, xf, p["w_out"], p["norm_ffn"], ffn["router"], tm, mm)
            xf = _moe(hn, x1, comb, slot, slot_t, moe["wg"], moe["wu"], moe["wd"], moe["norm_final"],
                      min(MOE_TM, n), _Matmul(precise=False))
    return xf.reshape(b, t, D_MODEL), new_states


def kernel(x_prompt, x_sample, state_delta, state_conv, state_ssm_re, state_ssm_im, meta_tokens, norm_mix, w_in, conv_w, a_log, dt_bias, norm_gate, ssm_a_re, ssm_a_im, ssm_b_re, ssm_b_im, ssm_c_re, ssm_c_im, ssm_d, ssm_log_dt, w_glu, b_glu, norm_ssm, w_out, norm_ffn, ffn_w_gate, ffn_w_up, ffn_w_down, router_w, moe_w_gate, moe_w_up, moe_w_down, norm_final):
    a = dict(norm_mix=norm_mix, w_in=w_in, conv_w=conv_w, a_log=a_log, dt_bias=dt_bias, norm_gate=norm_gate,
             ssm_a_re=ssm_a_re, ssm_a_im=ssm_a_im, ssm_b_re=ssm_b_re, ssm_b_im=ssm_b_im,
             ssm_c_re=ssm_c_re, ssm_c_im=ssm_c_im, ssm_d=ssm_d, ssm_log_dt=ssm_log_dt,
             w_glu=w_glu, b_glu=b_glu, norm_ssm=norm_ssm, w_out=w_out, norm_ffn=norm_ffn)
    depth = w_in.shape[0]

    layers32 = [_layer_params(l, a) for l in range(depth)]
    ffn32 = dict(wg=ffn_w_gate[0], wu=ffn_w_up[0], wd=ffn_w_down[0],
                 router=jnp.pad(router_w[0], ((0, 0), (0, LANES - N_EXPERTS))))
    moe = dict(wg=moe_w_gate[0].astype(BF16), wu=moe_w_up[0].astype(BF16), wd=moe_w_down[0].astype(BF16),
               norm_final=norm_final[None])

    def weights(mm):
        layers = [{k: (v.astype(mm.act) if k in MATMUL_WEIGHTS else v) for k, v in p.items()} for p in layers32]
        return layers, {k: v.astype(mm.act) for k, v in ffn32.items()}, moe

    bp = x_prompt.shape[0]
    bs, ts = x_sample.shape[0], x_sample.shape[1]
    side_b = 2 * MIX_BT
    pad = side_b - bs - 1
    x_side = jnp.concatenate([x_sample, meta_tokens[None], jnp.zeros((pad, ts, D_MODEL), F32)], axis=0)

    def side_state(st):
        zeros = jnp.zeros((side_b - bs,) + st.shape[1:], F32)
        return jnp.concatenate([st, zeros], axis=0)

    side_states = [(side_state(state_delta[l]), side_state(state_conv[l]),
                    side_state(state_ssm_re[l]), side_state(state_ssm_im[l])) for l in range(depth)]
    precise = _Matmul(precise=True)
    y_side, side_new = _trunk(x_side, side_states, *weights(precise), ts, side_b * ts, precise)

    def from_meta(st):
        return jnp.broadcast_to(st[bs:bs + 1], (bp,) + st.shape[1:])

    main_states = [tuple(from_meta(st) for st in side_new[l]) for l in range(depth)]
    fast = _Matmul(precise=False)
    y_prompt, main_new = _trunk(x_prompt, main_states, *weights(fast), CHUNK, 512, fast)

    def stack(new, idx, count):
        return jnp.stack([new[l][idx][:count] for l in range(depth)])

    return (y_prompt, y_side[:bs],
            stack(main_new, 0, bp), stack(main_new, 1, bp), stack(main_new, 2, bp), stack(main_new, 3, bp),
            stack(side_new, 0, bs), stack(side_new, 1, bs), stack(side_new, 2, bs), stack(side_new, 3, bs))
```

```python
import functools

import jax
import jax.numpy as jnp
from jax import lax
from jax.experimental import pallas as pl
from jax.experimental.pallas import tpu as pltpu

F32 = jnp.float32
BF16 = jnp.bfloat16

D_MODEL = 1024
N_META = 16
H_A = 4
DK = 128
DV = 128
W_A = H_A * DV
K_CONV = 4
QKV_DIM = 2 * H_A * DK + H_A * DV
W_B = D_MODEL - W_A
GROUP_CH = 16
G_B = W_B // GROUP_CH
P_STATE = 64
SSM_STATE = G_B * P_STATE
D_FF = 2816
N_EXPERTS = 8
D_FF_EXPERT = 1408
EPS = 1e-6

LANES = 128
SUBLANES = 8
PROJ_PAD = QKV_DIM + W_A + W_B + LANES
COL_Z = QKV_DIM
COL_U = QKV_DIM + W_A
COL_BA = QKV_DIM + W_A + W_B
CONV_PAD = SUBLANES
SSM_IN_BLOCKS = W_B // LANES
SSM_HALF_STATE = SSM_STATE // 2
MIX_BT = 8
DELTA_STACK = 2
CHUNK = 64
MOE_TM = 1024
MOE_SUB = LANES
MOE_CHUNK = 128
MOE_ALIGN = SUBLANES
MOE_WIN = MOE_SUB + 16
MOE_UNROUTED = -2.0 * MOE_WIN
VMEM_LIMIT = 60 * 1024 * 1024


class _Matmul:
    def __init__(self, precise):
        self.precise = precise
        self.act = F32 if precise else BF16

    def _args(self, a, b):
        if self.precise:
            return (a.astype(F32), b.astype(F32)), dict(precision=lax.Precision.HIGHEST)
        return (a.astype(BF16), b.astype(BF16)), {}

    def dot(self, a, b):
        args, kw = self._args(a, b)
        return jnp.dot(*args, preferred_element_type=F32, **kw)

    def einsum(self, spec, a, b):
        args, kw = self._args(a, b)
        return jnp.einsum(spec, *args, preferred_element_type=F32, **kw)


def _sigmoid(x):
    return 1.0 / (1.0 + jnp.exp(-x))


def _silu(x):
    return x * _sigmoid(x)


def _rms(x, gain):
    return x * lax.rsqrt(jnp.mean(x * x, axis=-1, keepdims=True) + EPS) * gain


def _resident(shape, grid_rank):
    zeros = (0,) * len(shape)
    index_map = (lambda i: zeros) if grid_rank == 1 else (lambda i, j: zeros)
    return pl.BlockSpec(shape, index_map, pipeline_mode=pl.Buffered(1))


def _delta_head(q, k, v, beta, gcum, s, mm):
    n, l, _ = q.shape
    ns = DELTA_STACK
    p = n // ns

    def parts(x):
        return [x[i * p:(i + 1) * p] for i in range(ns)]

    def side2(x):
        return jnp.concatenate(parts(x), axis=-1)

    def diag2(x):
        zero = jnp.zeros_like(x[:p])
        return jnp.concatenate([jnp.concatenate([xi if j == i else zero for j in range(ns)], axis=-1)
                                for i, xi in enumerate(parts(x))], axis=1)

    def unside2(x2, w):
        return jnp.concatenate([x2[..., i * w:(i + 1) * w] for i in range(ns)], axis=0)

    lane2 = lax.broadcasted_iota(jnp.int32, (p, l, ns * l), 2)
    block = sum((lane2 >= i * l).astype(jnp.int32) for i in range(1, ns))

    def per_row2(x):
        xs = parts(x)
        out = xs[ns - 1]
        for i in range(ns - 2, -1, -1):
            out = jnp.where(block == i, xs[i], out)
        return out

    def own_block2(x2):
        return jnp.concatenate([jnp.where(block == i, x2, 0.0) for i in range(ns)], axis=1)

    g_hi = gcum.astype(BF16).astype(F32)
    r1 = gcum - g_hi
    g_mid = r1.astype(BF16).astype(F32)
    g_lo = r1 - g_mid
    lane = lax.broadcasted_iota(jnp.int32, (n, l, LANES), 2)
    pieces = jnp.where(lane == 0, g_hi, jnp.where(lane == 1, g_mid, jnp.where(lane == 2, g_lo, 0.0)))
    ones = jnp.ones((p, l, ns * LANES), F32)
    g_row = mm.einsum("bik,bjk->bij", ones, diag2(pieces))
    g_col = per_row2(gcum)
    ii = lax.broadcasted_iota(jnp.int32, (p, l, ns * l), 1)
    jj = lane2 - block * l
    incl = ii >= jj
    strict = ii > jj
    decay = jnp.where(incl, jnp.exp(jnp.where(incl, g_col - g_row, 0.0)), 0.0)
    k_diag = diag2(k)
    kk = mm.einsum("bik,bjk->bij", side2(k), k_diag)
    qk = mm.einsum("bik,bjk->bij", side2(q), k_diag) * decay
    neg_m = jnp.where(strict, -(per_row2(beta) * kk * decay), 0.0)
    eye = jnp.where(ii == jj, 1.0, 0.0)
    t_inv = eye + neg_m
    pw = neg_m
    size = 2
    while size < l:
        pw = mm.einsum("bij,bjk->bik", pw, own_block2(pw))
        t_inv = t_inv + mm.einsum("bij,bjk->bik", t_inv, own_block2(pw))
        size *= 2
    eg = jnp.exp(gcum)
    rhs = jnp.concatenate([beta * v, (beta * eg) * k], axis=-1)
    sol = unside2(mm.einsum("bij,bjd->bid", t_inv, diag2(rhs)), 2 * DV)
    u0 = sol[..., :DV]
    w = sol[..., DV:]
    u = u0 - mm.einsum("bik,bkv->biv", w, s)
    o = mm.einsum("bik,bkv->biv", q * eg, s) + unside2(mm.einsum("bij,bjv->biv", qk, diag2(u)), DV)
    g_last = gcum[:, l - 1:l, :]
    kg = k * jnp.exp(g_last - gcum)
    s_new = s * jnp.exp(g_last) + mm.einsum("bjk,bjv->bkv", kg, u)
    return o, s_new


def _mixer_kernel(xfirst_ref, xnext_ref, gin_ref, win_ref, s0_ref, c0_ref, hr0_ref, hi0_ref,
                  convw_ref, alog_ref, dtb_ref, ngate_ref,
                  are_ref, aim_ref, bbre_ref, bbim_ref, ccre_ref, ccim_ref,
                  d_ref, wglu_ref, bglu_ref, nssm_ref,
                  out_ref, s_ref, c_ref, hr_ref, hi_ref,
                  proj_scr, xp_scr, xr_scr, xi_scr, *, bt, l, lookahead, mm):
    rows = bt * l
    chunk = pl.program_id(1)

    def project(x_ref):
        hn = _rms(x_ref[...].reshape(rows, D_MODEL), gin_ref[...])
        return mm.dot(hn, win_ref[...])

    @pl.when(chunk == 0)
    def _():
        s_ref[...] = s0_ref[...]
        hr_ref[...] = hr0_ref[...]
        hi_ref[...] = hi0_ref[...]
        xp_scr[:, CONV_PAD - (K_CONV - 1):CONV_PAD, :] = c0_ref[...]
        proj_scr[0:rows, :] = project(xfirst_ref)

    slot = chunk % 2
    cur = pl.ds(pl.multiple_of(slot * rows, rows), rows)

    def proj_cols(start, width):
        return proj_scr[cur, start:start + width].reshape(bt, l, width)

    xp_scr[:, CONV_PAD:CONV_PAD + l, :] = proj_cols(0, QKV_DIM)
    xp = xp_scr[...]
    acc = xp[:, CONV_PAD:, :] * convw_ref[K_CONV - 1:K_CONV, :]
    for back in range(1, K_CONV):
        shifted = pltpu.roll(xp, back, 1)[:, CONV_PAD:, :]
        acc = acc + shifted * convw_ref[K_CONV - 1 - back:K_CONV - back, :]
    tail = xp_scr[:, CONV_PAD + l - (K_CONV - 1):CONV_PAD + l, :]
    xp_scr[:, CONV_PAD - (K_CONV - 1):CONV_PAD, :] = tail
    c_ref[...] = tail
    qkv = _silu(acc)

    ba = proj_scr[cur, COL_BA:COL_BA + LANES]
    beta_all = _sigmoid(ba)
    sp_in = ba + dtb_ref[...]
    softplus = jnp.maximum(sp_in, 0.0) + jnp.log1p(jnp.exp(-jnp.abs(sp_in)))
    g_all = -jnp.exp(alog_ref[...]) * softplus
    t_idx = lax.broadcasted_iota(jnp.int32, (rows, LANES), 0) % l
    shift = 1
    while shift < l:
        g_all = g_all + jnp.where(t_idx >= shift, pltpu.roll(g_all, shift, 0), 0.0)
        shift *= 2
    beta_all = beta_all.reshape(bt, l, LANES)
    g_all = g_all.reshape(bt, l, LANES)

    def heads(first_col, width, src):
        return jnp.concatenate([src[:, :, first_col + h * width:first_col + (h + 1) * width]
                                for h in range(H_A)], axis=0)

    z = proj_cols(COL_Z, W_A)
    q = heads(0, DK, qkv)
    k = heads(H_A * DK, DK, qkv)
    v = heads(2 * H_A * DK, DV, qkv)
    q = q * lax.rsqrt(jnp.sum(q * q, axis=-1, keepdims=True) + EPS) * (DK ** -0.5)
    k = k * lax.rsqrt(jnp.sum(k * k, axis=-1, keepdims=True) + EPS)
    s_all = jnp.concatenate([s_ref[:, h] for h in range(H_A)], axis=0)
    o, s_new = _delta_head(q, k, v, heads(0, 1, beta_all), heads(H_A, 1, g_all), s_all, mm)
    o = _rms(o, ngate_ref[...]) * _silu(heads(0, DV, z))
    for h in range(H_A):
        s_ref[:, h] = s_new[h * bt:(h + 1) * bt]
        out_ref[:, :, h * DV:(h + 1) * DV] = o[h * bt:(h + 1) * bt].astype(out_ref.dtype)

    u_tb = pltpu.einshape("blc->lbc", proj_cols(COL_U, W_B)).reshape(rows, W_B)
    in_w = W_B // SSM_IN_BLOCKS
    st_w = SSM_STATE // SSM_IN_BLOCKS
    for blk in range(SSM_IN_BLOCKS):
        ub = u_tb[:, blk * in_w:(blk + 1) * in_w]
        cols = slice(blk * st_w, (blk + 1) * st_w)
        xr_scr[:, cols] = mm.dot(ub, bbre_ref[blk])
        xi_scr[:, cols] = mm.dot(ub, bbim_ref[blk])

    for half in range(2):
        cols = slice(half * SSM_HALF_STATE, (half + 1) * SSM_HALF_STATE)
        a_re = jnp.broadcast_to(are_ref[:, cols], (bt, SSM_HALF_STATE))
        a_im = jnp.broadcast_to(aim_ref[:, cols], (bt, SSM_HALF_STATE))

        h_re, h_im = hr_ref[:, cols], hi_ref[:, cols]
        for t in range(l):
            step_rows = slice(t * bt, (t + 1) * bt)
            h_re, h_im = (a_re * h_re - a_im * h_im + xr_scr[step_rows, cols],
                          a_re * h_im + a_im * h_re + xi_scr[step_rows, cols])
            xr_scr[step_rows, cols] = h_re
            xi_scr[step_rows, cols] = h_im
        hr_ref[:, cols] = h_re
        hi_ref[:, cols] = h_im

    y_parts = []
    for half in range(2):
        cols = slice(half * SSM_HALF_STATE, (half + 1) * SSM_HALF_STATE)
        y_parts.append(mm.dot(xr_scr[:, cols], ccre_ref[half]) - mm.dot(xi_scr[:, cols], ccim_ref[half]))
    y = jnp.concatenate(y_parts, axis=-1) + d_ref[...] * u_tb
    gy = 0.5 * y * (1.0 + jnp.tanh(0.7978845608028654 * (y + 0.044715 * (y * y * y))))
    ob = gy * _sigmoid(mm.dot(gy, wglu_ref[...]) + bglu_ref[...])
    ob = _rms(ob, nssm_ref[...])
    out_ref[:, :, W_A:] = pltpu.einshape("lbc->blc", ob.reshape(l, bt, W_B)).astype(out_ref.dtype)

    if lookahead:
        proj_scr[pl.ds(pl.multiple_of((1 - slot) * rows, rows), rows), :] = project(xnext_ref)


def _mixer(x, s0, c0, hr0, hi0, p, l, mm):
    b, t, _ = x.shape
    bt = MIX_BT
    rows = bt * l
    nchunks = t // l
    grid = (b // bt, nchunks)

    def state(shape, **kw):
        nd = len(shape)
        return pl.BlockSpec((bt,) + shape, lambda i, n: (i,) + (0,) * nd, **kw)

    def const(shape):
        return _resident(shape, 2)

    x_first = pl.BlockSpec((bt, l, D_MODEL), lambda i, n: (i, 0, 0), pipeline_mode=pl.Buffered(1))
    x_next = pl.BlockSpec((bt, l, D_MODEL), lambda i, n: (i, jnp.minimum(n + 1, nchunks - 1), 0))
    consts = [p["norm_mix"], p["w_in"]]
    consts_tail = [p["conv_w"], p["a_log_row"], p["dt_bias_row"], p["norm_gate"],
              p["a_re"], p["a_im"], p["bb_re"], p["bb_im"], p["cc_re"], p["cc_im"],
              p["ssm_d"], p["w_glu"], p["b_glu"], p["norm_ssm"]]
    state_shapes = [(H_A, DK, DV), (K_CONV - 1, QKV_DIM), (SSM_STATE,), (SSM_STATE,)]
    out_shapes = ([jax.ShapeDtypeStruct((b, t, D_MODEL), mm.act)]
                  + [jax.ShapeDtypeStruct((b,) + s, F32) for s in state_shapes])
    return pl.pallas_call(
        functools.partial(_mixer_kernel, bt=bt, l=l, lookahead=nchunks > 1, mm=mm),
        grid=grid,
        in_specs=([x_first, x_next] + [const(c.shape) for c in consts]
                  + [state(s, pipeline_mode=pl.Buffered(1)) for s in state_shapes]
                  + [const(c.shape) for c in consts_tail]),
        out_specs=[pl.BlockSpec((bt, l, D_MODEL), lambda i, n: (i, n, 0))] + [state(s) for s in state_shapes],
        out_shape=out_shapes,
        scratch_shapes=[pltpu.VMEM((2 * rows, PROJ_PAD), F32),
                        pltpu.VMEM((bt, CONV_PAD + l, QKV_DIM), F32),
                        pltpu.VMEM((rows, SSM_STATE), F32),
                        pltpu.VMEM((rows, SSM_STATE), F32)],
        compiler_params=pltpu.CompilerParams(dimension_semantics=("parallel", "arbitrary"),
                                             vmem_limit_bytes=VMEM_LIMIT),
        name="mixer",
    )(x, x, *consts, s0, c0, hr0, hi0, *consts_tail)


def _out_ffn_kernel(mix_ref, x_ref, wo_ref, g_ref, wg_ref, wu_ref, wd_ref, o_ref, *, mm):
    x1 = x_ref[...] + mm.dot(mix_ref[...], wo_ref[...])
    hn = _rms(x1, g_ref[...]).astype(mm.act)
    acc = x1
    for c in range(D_FF // D_FF_EXPERT):
        cols = slice(c * D_FF_EXPERT, (c + 1) * D_FF_EXPERT)
        hidden = _silu(mm.dot(hn, wg_ref[:, cols])) * mm.dot(hn, wu_ref[:, cols])
        acc = acc + mm.dot(hidden, wd_ref[cols, :])
    o_ref[...] = acc


def _out_ffn(mixed, x, w_out, gain, wg, wu, wd, tm, mm):
    n = x.shape[0]
    row = lambda w: pl.BlockSpec((tm, w), lambda i: (i, 0))
    const = lambda a: _resident(a.shape, 1)
    return pl.pallas_call(
        functools.partial(_out_ffn_kernel, mm=mm),
        grid=(pl.cdiv(n, tm),),
        in_specs=[row(D_MODEL), row(D_MODEL), const(w_out), const(gain), const(wg), const(wu), const(wd)],
        out_specs=row(D_MODEL),
        out_shape=jax.ShapeDtypeStruct((n, D_MODEL), F32),
        compiler_params=pltpu.CompilerParams(dimension_semantics=("parallel",),
                                             vmem_limit_bytes=VMEM_LIMIT),
        name="out_ffn",
    )(mixed, x, w_out, gain, wg, wu, wd)


def _out_router_kernel(mix_ref, x_ref, wo_ref, g_ref, wr_ref, x1_ref, hn_ref, comb_ref, slot_ref, slott_ref,
                       *, mm):
    x1 = x_ref[...] + mm.dot(mix_ref[...], wo_ref[...])
    x1_ref[...] = x1
    hn = _rms(x1, g_ref[...]).astype(mm.act)
    hn_ref[...] = hn
    logits = mm.dot(hn, wr_ref[...])
    lane = lax.broadcasted_iota(jnp.int32, logits.shape, 1).astype(F32)
    neg = jnp.float32(-jnp.inf)
    logits = jnp.where(lane < N_EXPERTS, logits, neg)
    v1 = jnp.max(logits, axis=-1, keepdims=True)
    i1 = jnp.min(jnp.where(logits == v1, lane, float(LANES)), axis=-1, keepdims=True)
    rest = jnp.where(lane == i1, neg, logits)
    v2 = jnp.max(rest, axis=-1, keepdims=True)
    i2 = jnp.min(jnp.where(rest == v2, lane, float(LANES)), axis=-1, keepdims=True)
    e2 = jnp.exp(v2 - v1)
    den = 1.0 + e2
    comb = jnp.where(lane == i1, 1.0 / den, jnp.where(lane == i2, e2 / den, 0.0))
    comb_ref[...] = comb
    routed = comb > 0.0
    routed_t = routed.astype(F32).T[:N_EXPERTS, :] > 0.0
    r_i = lax.broadcasted_iota(jnp.int32, (MOE_SUB, MOE_SUB), 0)
    c_i = lax.broadcasted_iota(jnp.int32, (MOE_SUB, MOE_SUB), 1)
    earlier_rows = jnp.where(c_i < r_i, 1.0, 0.0).astype(BF16)
    earlier_cols = jnp.where(r_i < c_i, 1.0, 0.0).astype(BF16)
    for s in range(comb.shape[0] // MOE_SUB):
        rows = slice(s * MOE_SUB, (s + 1) * MOE_SUB)
        rank = jnp.dot(earlier_rows, jnp.where(routed[rows], 1.0, 0.0).astype(BF16),
                       preferred_element_type=F32)
        slot_ref[rows, :] = jnp.where(routed[rows], rank, MOE_UNROUTED)
        rank_t = jnp.dot(jnp.where(routed_t[:, rows], 1.0, 0.0).astype(BF16), earlier_cols,
                         preferred_element_type=F32)
        slott_ref[:, rows] = jnp.where(routed_t[:, rows], rank_t, MOE_UNROUTED)


def _out_router(mixed, x, w_out, gain, w_router, tm, mm):
    n = x.shape[0]
    row = lambda w: pl.BlockSpec((tm, w), lambda i: (i, 0))
    const = lambda a: _resident(a.shape, 1)
    return pl.pallas_call(
        functools.partial(_out_router_kernel, mm=mm),
        grid=(pl.cdiv(n, tm),),
        in_specs=[row(D_MODEL), row(D_MODEL), const(w_out), const(gain), const(w_router)],
        out_specs=[row(D_MODEL), row(D_MODEL), row(LANES), row(LANES),
                   pl.BlockSpec((N_EXPERTS, tm), lambda i: (0, i))],
        out_shape=[jax.ShapeDtypeStruct((n, D_MODEL), F32),
                   jax.ShapeDtypeStruct((n, D_MODEL), mm.act),
                   jax.ShapeDtypeStruct((n, LANES), F32),
                   jax.ShapeDtypeStruct((n, LANES), F32),
                   jax.ShapeDtypeStruct((N_EXPERTS, n), F32)],
        compiler_params=pltpu.CompilerParams(dimension_semantics=("parallel",),
                                             vmem_limit_bytes=VMEM_LIMIT),
        name="out_router",
    )(mixed, x, w_out, gain, w_router)


def _moe_kernel(cnt_ref, hn_ref, x1_ref, comb_ref, slot_ref, slott_ref, wg_ref, wu_ref, wd_ref, gf_ref, o_ref,
                acc_ref, xc_ref, y_ref, *, tm, mm):
    i = pl.program_id(0)
    e = pl.program_id(1)
    nsub = tm // MOE_SUB

    @pl.when((i == 0) & (e == 0))
    def _():
        xc_ref[...] = jnp.zeros_like(xc_ref)
        y_ref[...] = jnp.zeros_like(y_ref)

    @pl.when(e == 0)
    def _():
        acc_ref[...] = jnp.zeros_like(acc_ref)

    lane = lax.broadcasted_iota(jnp.int32, (MOE_SUB, LANES), 1)
    expert_row = lax.broadcasted_iota(jnp.int32, (N_EXPERTS, MOE_SUB), 0) == e

    spans = []
    base = jnp.int32(0)
    for s in range(nsub):
        start = (base // MOE_ALIGN) * MOE_ALIGN
        spans.append((pl.multiple_of(start, MOE_ALIGN), (base - start).astype(F32)))
        base = base + cnt_ref[(i * nsub + s) * N_EXPERTS + e]
    total = base

    slot_row = lax.broadcasted_iota(jnp.int32, (MOE_WIN, MOE_SUB), 0).astype(F32)
    for s, (start, shift) in enumerate(spans):
        rows = slice(s * MOE_SUB, (s + 1) * MOE_SUB)
        slot = jnp.sum(jnp.where(expert_row, slott_ref[:, rows], 0.0), axis=0, keepdims=True)
        onehot = jnp.where(slot + shift == slot_row, 1.0, 0.0).astype(BF16)
        packed = mm.dot(onehot, hn_ref[rows, :])
        if s == 0:
            xc_ref[pl.ds(start, MOE_ALIGN), :] = packed[:MOE_ALIGN]
        else:
            xc_ref[pl.ds(start, MOE_ALIGN), :] += packed[:MOE_ALIGN]
        xc_ref[pl.ds(start + MOE_ALIGN, MOE_WIN - MOE_ALIGN), :] = packed[MOE_ALIGN:]

    def ffn(c, carry):
        r = pl.ds(pl.multiple_of(c * MOE_CHUNK, MOE_CHUNK), MOE_CHUNK)
        x = xc_ref[r, :].astype(mm.act)
        hidden = _silu(mm.dot(x, wg_ref[0])) * mm.dot(x, wu_ref[0])
        y_ref[r, :] = mm.dot(hidden, wd_ref[0])
        return carry

    lax.fori_loop(0, (total + MOE_CHUNK - 1) // MOE_CHUNK, ffn, 0)

    slot_lane = lax.broadcasted_iota(jnp.int32, (MOE_SUB, MOE_WIN), 1).astype(F32)
    for s, (start, shift) in enumerate(spans):
        rows = slice(s * MOE_SUB, (s + 1) * MOE_SUB)
        gate = jnp.sum(jnp.where(lane == e, comb_ref[rows, :], 0.0), axis=-1, keepdims=True)
        slot = jnp.sum(jnp.where(lane == e, slot_ref[rows, :], 0.0), axis=-1, keepdims=True)
        onehot = jnp.where(slot + shift == slot_lane, 1.0, 0.0)
        acc_ref[rows, :] += gate * mm.dot(onehot, y_ref[pl.ds(start, MOE_WIN), :])

    @pl.when(e == N_EXPERTS - 1)
    def _():
        o_ref[...] = _rms(x1_ref[...] + acc_ref[...], gf_ref[...])


def _moe(hn, x1, comb, slot, slot_t, wg, wu, wd, gain_final, tm, mm):
    n = x1.shape[0]
    assert n % tm == 0 and tm % MOE_SUB == 0
    routed = comb[:, :N_EXPERTS] > 0.0
    counts = jnp.sum(routed.reshape(n // MOE_SUB, MOE_SUB, N_EXPERTS), axis=1, dtype=jnp.int32).reshape(-1)
    row = lambda w: pl.BlockSpec((tm, w), lambda i, e, cnt: (i, 0))
    expert = lambda a, b: pl.BlockSpec((1, a, b), lambda i, e, cnt: (e, 0, 0))
    packed_rows = tm + MOE_WIN + MOE_CHUNK
    return pl.pallas_call(
        functools.partial(_moe_kernel, tm=tm, mm=mm),
        grid_spec=pltpu.PrefetchScalarGridSpec(
            num_scalar_prefetch=1,
            grid=(n // tm, N_EXPERTS),
            in_specs=[row(D_MODEL), row(D_MODEL), row(LANES), row(LANES),
                      pl.BlockSpec((N_EXPERTS, tm), lambda i, e, cnt: (0, i)),
                      expert(D_MODEL, D_FF_EXPERT), expert(D_MODEL, D_FF_EXPERT), expert(D_FF_EXPERT, D_MODEL),
                      pl.BlockSpec((1, D_MODEL), lambda i, e, cnt: (0, 0))],
            out_specs=row(D_MODEL),
            scratch_shapes=[pltpu.VMEM((tm, D_MODEL), F32),
                            pltpu.VMEM((packed_rows, D_MODEL), F32),
                            pltpu.VMEM((packed_rows, D_MODEL), F32)]),
        out_shape=jax.ShapeDtypeStruct((n, D_MODEL), F32),
        compiler_params=pltpu.CompilerParams(dimension_semantics=("arbitrary", "arbitrary"),
                                             vmem_limit_bytes=VMEM_LIMIT),
        name="moe",
    )(counts, hn, x1, comb, slot, slot_t, wg, wu, wd, gain_final)


def _lane_row(vec, offset):
    return jnp.zeros((1, LANES), F32).at[0, offset:offset + vec.shape[0]].set(vec.astype(F32))


def _block_diag(w, nblk):
    g, a, b = w.shape
    hg = g // nblk
    w = w.reshape(nblk, hg, a, b)
    eye = jnp.eye(hg, dtype=w.dtype)
    return jnp.einsum("sgab,gh->sgahb", w, eye).reshape(nblk, hg * a, hg * b)


MATMUL_WEIGHTS = ("w_in", "bb_re", "bb_im", "cc_re", "cc_im", "w_glu", "w_out")


def _layer_params(l, a):
    w_in = a["w_in"][l]
    w_cat = jnp.concatenate([w_in[:, :QKV_DIM + W_A], w_in[:, QKV_DIM + W_A + 2 * H_A:],
                             w_in[:, QKV_DIM + W_A:QKV_DIM + W_A + 2 * H_A],
                             jnp.zeros((D_MODEL, LANES - 2 * H_A), F32)], axis=1)
    lam_re = a["ssm_a_re"][l]
    lam_im = a["ssm_a_im"][l]
    delta = jnp.exp(a["ssm_log_dt"][l])[:, None]
    mag = jnp.exp(lam_re * delta)
    ab_re = mag * jnp.cos(lam_im * delta)
    ab_im = mag * jnp.sin(lam_im * delta)
    den = lam_re * lam_re + lam_im * lam_im
    f_re = ((ab_re - 1.0) * lam_re + ab_im * lam_im) / den
    f_im = (ab_im * lam_re - (ab_re - 1.0) * lam_im) / den
    b_re = a["ssm_b_re"][l]
    b_im = a["ssm_b_im"][l]
    bb_re = f_re[..., None] * b_re - f_im[..., None] * b_im
    bb_im = f_re[..., None] * b_im + f_im[..., None] * b_re
    return dict(
        norm_mix=a["norm_mix"][l][None], w_in=w_cat,
        conv_w=a["conv_w"][l],
        a_log_row=_lane_row(a["a_log"][l], H_A), dt_bias_row=_lane_row(a["dt_bias"][l], H_A),
        norm_gate=a["norm_gate"][l][None],
        a_re=ab_re.reshape(1, SSM_STATE), a_im=ab_im.reshape(1, SSM_STATE),
        bb_re=_block_diag(jnp.swapaxes(bb_re, 1, 2), SSM_IN_BLOCKS),
        bb_im=_block_diag(jnp.swapaxes(bb_im, 1, 2), SSM_IN_BLOCKS),
        cc_re=_block_diag(jnp.swapaxes(a["ssm_c_re"][l], 1, 2), 2),
        cc_im=_block_diag(jnp.swapaxes(a["ssm_c_im"][l], 1, 2), 2),
        ssm_d=a["ssm_d"][l][None], w_glu=a["w_glu"][l], b_glu=a["b_glu"][l][None],
        norm_ssm=a["norm_ssm"][l][None],
        w_out=a["w_out"][l], norm_ffn=a["norm_ffn"][l][None])


def _trunk(x, states, layers, ffn, moe, l, tm, mm):
    b, t, _ = x.shape
    n = b * t
    xf = x.reshape(n, D_MODEL)
    new_states = []
    for li, p in enumerate(layers):
        s0, c0, hr0, hi0 = states[li]
        mixed, s1, c1, hr1, hi1 = _mixer(xf.reshape(b, t, D_MODEL), s0, c0, hr0.reshape(b, SSM_STATE),
                                         hi0.reshape(b, SSM_STATE), p, l, mm)
        new_states.append((s1, c1, hr1.reshape(b, G_B, P_STATE), hi1.reshape(b, G_B, P_STATE)))
        mixed = mixed.reshape(n, D_MODEL)
        if li == 0:
            xf = _out_ffn(mixed, xf, p["w_out"], p["norm_ffn"], ffn["wg"], ffn["wu"], ffn["wd"], tm, mm)
        else:
            x1, hn, comb, slot, slot_t = _out_router(mixed, xf, p["w_out"], p["norm_ffn"], ffn["router"], tm, mm)
            xf = _moe(hn, x1, comb, slot, slot_t, moe["wg"], moe["wu"], moe["wd"], moe["norm_final"],
                      min(MOE_TM, n), _Matmul(precise=False))
    return xf.reshape(b, t, D_MODEL), new_states


def kernel(x_prompt, x_sample, state_delta, state_conv, state_ssm_re, state_ssm_im, meta_tokens, norm_mix, w_in, conv_w, a_log, dt_bias, norm_gate, ssm_a_re, ssm_a_im, ssm_b_re, ssm_b_im, ssm_c_re, ssm_c_im, ssm_d, ssm_log_dt, w_glu, b_glu, norm_ssm, w_out, norm_ffn, ffn_w_gate, ffn_w_up, ffn_w_down, router_w, moe_w_gate, moe_w_up, moe_w_down, norm_final):
    a = dict(norm_mix=norm_mix, w_in=w_in, conv_w=conv_w, a_log=a_log, dt_bias=dt_bias, norm_gate=norm_gate,
             ssm_a_re=ssm_a_re, ssm_a_im=ssm_a_im, ssm_b_re=ssm_b_re, ssm_b_im=ssm_b_im,
             ssm_c_re=ssm_c_re, ssm_c_im=ssm_c_im, ssm_d=ssm_d, ssm_log_dt=ssm_log_dt,
             w_glu=w_glu, b_glu=b_glu, norm_ssm=norm_ssm, w_out=w_out, norm_ffn=norm_ffn)
    depth = w_in.shape[0]

    layers32 = [_layer_params(l, a) for l in range(depth)]
    ffn32 = dict(wg=ffn_w_gate[0], wu=ffn_w_up[0], wd=ffn_w_down[0],
                 router=jnp.pad(router_w[0], ((0, 0), (0, LANES - N_EXPERTS))))
    moe = dict(wg=moe_w_gate[0].astype(BF16), wu=moe_w_up[0].astype(BF16), wd=moe_w_down[0].astype(BF16),
               norm_final=norm_final[None])

    def weights(mm):
        layers = [{k: (v.astype(mm.act) if k in MATMUL_WEIGHTS else v) for k, v in p.items()} for p in layers32]
        return layers, {k: v.astype(mm.act) for k, v in ffn32.items()}, moe

    bp = x_prompt.shape[0]
    bs, ts = x_sample.shape[0], x_sample.shape[1]
    side_b = 2 * MIX_BT
    pad = side_b - bs - 1
    x_side = jnp.concatenate([x_sample, meta_tokens[None], jnp.zeros((pad, ts, D_MODEL), F32)], axis=0)

    def side_state(st):
        zeros = jnp.zeros((side_b - bs,) + st.shape[1:], F32)
        return jnp.concatenate([st, zeros], axis=0)

    side_states = [(side_state(state_delta[l]), side_state(state_conv[l]),
                    side_state(state_ssm_re[l]), side_state(state_ssm_im[l])) for l in range(depth)]
    precise = _Matmul(precise=True)
    y_side, side_new = _trunk(x_side, side_states, *weights(precise), ts, side_b * ts, precise)

    def from_meta(st):
        return jnp.broadcast_to(st[bs:bs + 1], (bp,) + st.shape[1:])

    main_states = [tuple(from_meta(st) for st in side_new[l]) for l in range(depth)]
    fast = _Matmul(precise=False)
    y_prompt, main_new = _trunk(x_prompt, main_states, *weights(fast), CHUNK, 512, fast)

    def stack(new, idx, count):
        return jnp.stack([new[l][idx][:count] for l in range(depth)])

    return (y_prompt, y_side[:bs],
            stack(main_new, 0, bp), stack(main_new, 1, bp), stack(main_new, 2, bp), stack(main_new, 3, bp),
            stack(side_new, 0, bs), stack(side_new, 1, bs), stack(side_new, 2, bs), stack(side_new, 3, bs))
```

```python
import functools

import jax
import jax.numpy as jnp
from jax import lax
from jax.experimental import pallas as pl
from jax.experimental.pallas import tpu as pltpu

F32 = jnp.float32
BF16 = jnp.bfloat16

D_MODEL = 1024
N_META = 16
H_A = 4
DK = 128
DV = 128
W_A = H_A * DV
K_CONV = 4
QKV_DIM = 2 * H_A * DK + H_A * DV
W_B = D_MODEL - W_A
GROUP_CH = 16
G_B = W_B // GROUP_CH
P_STATE = 64
SSM_STATE = G_B * P_STATE
D_FF = 2816
N_EXPERTS = 8
D_FF_EXPERT = 1408
EPS = 1e-6

LANES = 128
SUBLANES = 8
PROJ_PAD = QKV_DIM + W_A + W_B + LANES
COL_Z = QKV_DIM
COL_U = QKV_DIM + W_A
COL_BA = QKV_DIM + W_A + W_B
CONV_PAD = SUBLANES
SSM_IN_BLOCKS = W_B // LANES
SSM_HALF_STATE = SSM_STATE // 2
MIX_BT = 8
DELTA_STACK = 2
CHUNK = 64
MOE_TM = 1024
MOE_SUB = LANES
MOE_CHUNK = 144
MOE_ALIGN = SUBLANES
MOE_WIN = MOE_SUB + 16
MOE_UNROUTED = -2.0 * MOE_WIN
VMEM_LIMIT = 60 * 1024 * 1024


class _Matmul:
    def __init__(self, precise):
        self.precise = precise
        self.act = F32 if precise else BF16

    def _args(self, a, b):
        if self.precise:
            return (a.astype(F32), b.astype(F32)), dict(precision=lax.Precision.HIGHEST)
        return (a.astype(BF16), b.astype(BF16)), {}

    def dot(self, a, b):
        args, kw = self._args(a, b)
        return jnp.dot(*args, preferred_element_type=F32, **kw)

    def einsum(self, spec, a, b):
        args, kw = self._args(a, b)
        return jnp.einsum(spec, *args, preferred_element_type=F32, **kw)


def _sigmoid(x):
    return 1.0 / (1.0 + jnp.exp(-x))


def _silu(x):
    return x * _sigmoid(x)


def _rms(x, gain):
    return x * lax.rsqrt(jnp.mean(x * x, axis=-1, keepdims=True) + EPS) * gain


def _resident(shape, grid_rank):
    zeros = (0,) * len(shape)
    index_map = (lambda i: zeros) if grid_rank == 1 else (lambda i, j: zeros)
    return pl.BlockSpec(shape, index_map, pipeline_mode=pl.Buffered(1))


def _delta_head(q, k, v, beta, gcum, s, mm):
    n, l, _ = q.shape
    ns = DELTA_STACK
    p = n // ns

    def parts(x):
        return [x[i * p:(i + 1) * p] for i in range(ns)]

    def side2(x):
        return jnp.concatenate(parts(x), axis=-1)

    def diag2(x):
        zero = jnp.zeros_like(x[:p])
        return jnp.concatenate([jnp.concatenate([xi if j == i else zero for j in range(ns)], axis=-1)
                                for i, xi in enumerate(parts(x))], axis=1)

    def unside2(x2, w):
        return jnp.concatenate([x2[..., i * w:(i + 1) * w] for i in range(ns)], axis=0)

    lane2 = lax.broadcasted_iota(jnp.int32, (p, l, ns * l), 2)
    block = sum((lane2 >= i * l).astype(jnp.int32) for i in range(1, ns))

    def per_row2(x):
        xs = parts(x)
        out = xs[ns - 1]
        for i in range(ns - 2, -1, -1):
            out = jnp.where(block == i, xs[i], out)
        return out

    def own_block2(x2):
        return jnp.concatenate([jnp.where(block == i, x2, 0.0) for i in range(ns)], axis=1)

    g_hi = gcum.astype(BF16).astype(F32)
    r1 = gcum - g_hi
    g_mid = r1.astype(BF16).astype(F32)
    g_lo = r1 - g_mid
    lane = lax.broadcasted_iota(jnp.int32, (n, l, LANES), 2)
    pieces = jnp.where(lane == 0, g_hi, jnp.where(lane == 1, g_mid, jnp.where(lane == 2, g_lo, 0.0)))
    ones = jnp.ones((p, l, ns * LANES), F32)
    g_row = mm.einsum("bik,bjk->bij", ones, diag2(pieces))
    g_col = per_row2(gcum)
    ii = lax.broadcasted_iota(jnp.int32, (p, l, ns * l), 1)
    jj = lane2 - block * l
    incl = ii >= jj
    strict = ii > jj
    decay = jnp.where(incl, jnp.exp(jnp.where(incl, g_col - g_row, 0.0)), 0.0)
    k_diag = diag2(k)
    kk = mm.einsum("bik,bjk->bij", side2(k), k_diag)
    qk = mm.einsum("bik,bjk->bij", side2(q), k_diag) * decay
    neg_m = jnp.where(strict, -(per_row2(beta) * kk * decay), 0.0)
    eye = jnp.where(ii == jj, 1.0, 0.0)
    t_inv = eye + neg_m
    pw = neg_m
    size = 2
    while size < l:
        pw = mm.einsum("bij,bjk->bik", pw, own_block2(pw))
        t_inv = t_inv + mm.einsum("bij,bjk->bik", t_inv, own_block2(pw))
        size *= 2
    eg = jnp.exp(gcum)
    rhs = jnp.concatenate([beta * v, (beta * eg) * k], axis=-1)
    sol = unside2(mm.einsum("bij,bjd->bid", t_inv, diag2(rhs)), 2 * DV)
    u0 = sol[..., :DV]
    w = sol[..., DV:]
    u = u0 - mm.einsum("bik,bkv->biv", w, s)
    o = mm.einsum("bik,bkv->biv", q * eg, s) + unside2(mm.einsum("bij,bjv->biv", qk, diag2(u)), DV)
    g_last = gcum[:, l - 1:l, :]
    kg = k * jnp.exp(g_last - gcum)
    s_new = s * jnp.exp(g_last) + mm.einsum("bjk,bjv->bkv", kg, u)
    return o, s_new


def _mixer_kernel(xfirst_ref, xnext_ref, gin_ref, win_ref, s0_ref, c0_ref, hr0_ref, hi0_ref,
                  convw_ref, alog_ref, dtb_ref, ngate_ref,
                  are_ref, aim_ref, bbre_ref, bbim_ref, ccre_ref, ccim_ref,
                  d_ref, wglu_ref, bglu_ref, nssm_ref,
                  out_ref, s_ref, c_ref, hr_ref, hi_ref,
                  proj_scr, xp_scr, xr_scr, xi_scr, *, bt, l, lookahead, mm):
    rows = bt * l
    chunk = pl.program_id(1)

    def project(x_ref):
        hn = _rms(x_ref[...].reshape(rows, D_MODEL), gin_ref[...])
        return mm.dot(hn, win_ref[...])

    @pl.when(chunk == 0)
    def _():
        s_ref[...] = s0_ref[...]
        hr_ref[...] = hr0_ref[...]
        hi_ref[...] = hi0_ref[...]
        xp_scr[:, CONV_PAD - (K_CONV - 1):CONV_PAD, :] = c0_ref[...]
        proj_scr[0:rows, :] = project(xfirst_ref)

    slot = chunk % 2
    cur = pl.ds(pl.multiple_of(slot * rows, rows), rows)

    def proj_cols(start, width):
        return proj_scr[cur, start:start + width].reshape(bt, l, width)

    xp_scr[:, CONV_PAD:CONV_PAD + l, :] = proj_cols(0, QKV_DIM)
    xp = xp_scr[...]
    acc = xp[:, CONV_PAD:, :] * convw_ref[K_CONV - 1:K_CONV, :]
    for back in range(1, K_CONV):
        shifted = pltpu.roll(xp, back, 1)[:, CONV_PAD:, :]
        acc = acc + shifted * convw_ref[K_CONV - 1 - back:K_CONV - back, :]
    tail = xp_scr[:, CONV_PAD + l - (K_CONV - 1):CONV_PAD + l, :]
    xp_scr[:, CONV_PAD - (K_CONV - 1):CONV_PAD, :] = tail
    c_ref[...] = tail
    qkv = _silu(acc)

    ba = proj_scr[cur, COL_BA:COL_BA + LANES]
    beta_all = _sigmoid(ba)
    sp_in = ba + dtb_ref[...]
    softplus = jnp.maximum(sp_in, 0.0) + jnp.log1p(jnp.exp(-jnp.abs(sp_in)))
    g_all = -jnp.exp(alog_ref[...]) * softplus
    t_idx = lax.broadcasted_iota(jnp.int32, (rows, LANES), 0) % l
    shift = 1
    while shift < l:
        g_all = g_all + jnp.where(t_idx >= shift, pltpu.roll(g_all, shift, 0), 0.0)
        shift *= 2
    beta_all = beta_all.reshape(bt, l, LANES)
    g_all = g_all.reshape(bt, l, LANES)

    def heads(first_col, width, src):
        return jnp.concatenate([src[:, :, first_col + h * width:first_col + (h + 1) * width]
                                for h in range(H_A)], axis=0)

    z = proj_cols(COL_Z, W_A)
    q = heads(0, DK, qkv)
    k = heads(H_A * DK, DK, qkv)
    v = heads(2 * H_A * DK, DV, qkv)
    q = q * lax.rsqrt(jnp.sum(q * q, axis=-1, keepdims=True) + EPS) * (DK ** -0.5)
    k = k * lax.rsqrt(jnp.sum(k * k, axis=-1, keepdims=True) + EPS)
    s_all = jnp.concatenate([s_ref[:, h] for h in range(H_A)], axis=0)
    o, s_new = _delta_head(q, k, v, heads(0, 1, beta_all), heads(H_A, 1, g_all), s_all, mm)
    o = _rms(o, ngate_ref[...]) * _silu(heads(0, DV, z))
    for h in range(H_A):
        s_ref[:, h] = s_new[h * bt:(h + 1) * bt]
        out_ref[:, :, h * DV:(h + 1) * DV] = o[h * bt:(h + 1) * bt].astype(out_ref.dtype)

    u_tb = pltpu.einshape("blc->lbc", proj_cols(COL_U, W_B)).reshape(rows, W_B)
    in_w = W_B // SSM_IN_BLOCKS
    st_w = SSM_STATE // SSM_IN_BLOCKS
    for blk in range(SSM_IN_BLOCKS):
        ub = u_tb[:, blk * in_w:(blk + 1) * in_w]
        cols = slice(blk * st_w, (blk + 1) * st_w)
        xr_scr[:, cols] = mm.dot(ub, bbre_ref[blk])
        xi_scr[:, cols] = mm.dot(ub, bbim_ref[blk])

    for half in range(2):
        cols = slice(half * SSM_HALF_STATE, (half + 1) * SSM_HALF_STATE)
        a_re = jnp.broadcast_to(are_ref[:, cols], (bt, SSM_HALF_STATE))
        a_im = jnp.broadcast_to(aim_ref[:, cols], (bt, SSM_HALF_STATE))

        h_re, h_im = hr_ref[:, cols], hi_ref[:, cols]
        for t in range(l):
            step_rows = slice(t * bt, (t + 1) * bt)
            h_re, h_im = (a_re * h_re - a_im * h_im + xr_scr[step_rows, cols],
                          a_re * h_im + a_im * h_re + xi_scr[step_rows, cols])
            xr_scr[step_rows, cols] = h_re
            xi_scr[step_rows, cols] = h_im
        hr_ref[:, cols] = h_re
        hi_ref[:, cols] = h_im

    y_parts = []
    for half in range(2):
        cols = slice(half * SSM_HALF_STATE, (half + 1) * SSM_HALF_STATE)
        y_parts.append(mm.dot(xr_scr[:, cols], ccre_ref[half]) - mm.dot(xi_scr[:, cols], ccim_ref[half]))
    y = jnp.concatenate(y_parts, axis=-1) + d_ref[...] * u_tb
    gy = 0.5 * y * (1.0 + jnp.tanh(0.7978845608028654 * (y + 0.044715 * (y * y * y))))
    ob = gy * _sigmoid(mm.dot(gy, wglu_ref[...]) + bglu_ref[...])
    ob = _rms(ob, nssm_ref[...])
    out_ref[:, :, W_A:] = pltpu.einshape("lbc->blc", ob.reshape(l, bt, W_B)).astype(out_ref.dtype)

    if lookahead:
        proj_scr[pl.ds(pl.multiple_of((1 - slot) * rows, rows), rows), :] = project(xnext_ref)


def _mixer(x, s0, c0, hr0, hi0, p, l, mm):
    b, t, _ = x.shape
    bt = MIX_BT
    rows = bt * l
    nchunks = t // l
    grid = (b // bt, nchunks)

    def state(shape, **kw):
        nd = len(shape)
        return pl.BlockSpec((bt,) + shape, lambda i, n: (i,) + (0,) * nd, **kw)

    def const(shape):
        return _resident(shape, 2)

    x_first = pl.BlockSpec((bt, l, D_MODEL), lambda i, n: (i, 0, 0), pipeline_mode=pl.Buffered(1))
    x_next = pl.BlockSpec((bt, l, D_MODEL), lambda i, n: (i, jnp.minimum(n + 1, nchunks - 1), 0))
    consts = [p["norm_mix"], p["w_in"]]
    consts_tail = [p["conv_w"], p["a_log_row"], p["dt_bias_row"], p["norm_gate"],
              p["a_re"], p["a_im"], p["bb_re"], p["bb_im"], p["cc_re"], p["cc_im"],
              p["ssm_d"], p["w_glu"], p["b_glu"], p["norm_ssm"]]
    state_shapes = [(H_A, DK, DV), (K_CONV - 1, QKV_DIM), (SSM_STATE,), (SSM_STATE,)]
    out_shapes = ([jax.ShapeDtypeStruct((b, t, D_MODEL), mm.act)]
                  + [jax.ShapeDtypeStruct((b,) + s, F32) for s in state_shapes])
    return pl.pallas_call(
        functools.partial(_mixer_kernel, bt=bt, l=l, lookahead=nchunks > 1, mm=mm),
        grid=grid,
        in_specs=([x_first, x_next] + [const(c.shape) for c in consts]
                  + [state(s, pipeline_mode=pl.Buffered(1)) for s in state_shapes]
                  + [const(c.shape) for c in consts_tail]),
        out_specs=[pl.BlockSpec((bt, l, D_MODEL), lambda i, n: (i, n, 0))] + [state(s) for s in state_shapes],
        out_shape=out_shapes,
        scratch_shapes=[pltpu.VMEM((2 * rows, PROJ_PAD), F32),
                        pltpu.VMEM((bt, CONV_PAD + l, QKV_DIM), F32),
                        pltpu.VMEM((rows, SSM_STATE), F32),
                        pltpu.VMEM((rows, SSM_STATE), F32)],
        compiler_params=pltpu.CompilerParams(dimension_semantics=("parallel", "arbitrary"),
                                             vmem_limit_bytes=VMEM_LIMIT),
        name="mixer",
    )(x, x, *consts, s0, c0, hr0, hi0, *consts_tail)


def _out_ffn_kernel(mix_ref, x_ref, wo_ref, g_ref, wg_ref, wu_ref, wd_ref, o_ref, *, mm):
    x1 = x_ref[...] + mm.dot(mix_ref[...], wo_ref[...])
    hn = _rms(x1, g_ref[...]).astype(mm.act)
    acc = x1
    for c in range(D_FF // D_FF_EXPERT):
        cols = slice(c * D_FF_EXPERT, (c + 1) * D_FF_EXPERT)
        hidden = _silu(mm.dot(hn, wg_ref[:, cols])) * mm.dot(hn, wu_ref[:, cols])
        acc = acc + mm.dot(hidden, wd_ref[cols, :])
    o_ref[...] = acc


def _out_ffn(mixed, x, w_out, gain, wg, wu, wd, tm, mm):
    n = x.shape[0]
    row = lambda w: pl.BlockSpec((tm, w), lambda i: (i, 0))
    const = lambda a: _resident(a.shape, 1)
    return pl.pallas_call(
        functools.partial(_out_ffn_kernel, mm=mm),
        grid=(pl.cdiv(n, tm),),
        in_specs=[row(D_MODEL), row(D_MODEL), const(w_out), const(gain), const(wg), const(wu), const(wd)],
        out_specs=row(D_MODEL),
        out_shape=jax.ShapeDtypeStruct((n, D_MODEL), F32),
        compiler_params=pltpu.CompilerParams(dimension_semantics=("parallel",),
                                             vmem_limit_bytes=VMEM_LIMIT),
        name="out_ffn",
    )(mixed, x, w_out, gain, wg, wu, wd)


def _out_router_kernel(mix_ref, x_ref, wo_ref, g_ref, wr_ref, x1_ref, hn_ref, comb_ref, slot_ref, slott_ref,
                       *, mm):
    x1 = x_ref[...] + mm.dot(mix_ref[...], wo_ref[...])
    x1_ref[...] = x1
    hn = _rms(x1, g_ref[...]).astype(mm.act)
    hn_ref[...] = hn
    logits = mm.dot(hn, wr_ref[...])
    lane = lax.broadcasted_iota(jnp.int32, logits.shape, 1).astype(F32)
    neg = jnp.float32(-jnp.inf)
    logits = jnp.where(lane < N_EXPERTS, logits, neg)
    v1 = jnp.max(logits, axis=-1, keepdims=True)
    i1 = jnp.min(jnp.where(logits == v1, lane, float(LANES)), axis=-1, keepdims=True)
    rest = jnp.where(lane == i1, neg, logits)
    v2 = jnp.max(rest, axis=-1, keepdims=True)
    i2 = jnp.min(jnp.where(rest == v2, lane, float(LANES)), axis=-1, keepdims=True)
    e2 = jnp.exp(v2 - v1)
    den = 1.0 + e2
    comb = jnp.where(lane == i1, 1.0 / den, jnp.where(lane == i2, e2 / den, 0.0))
    comb_ref[...] = comb
    routed = comb > 0.0
    routed_t = routed.astype(F32).T[:N_EXPERTS, :] > 0.0
    r_i = lax.broadcasted_iota(jnp.int32, (MOE_SUB, MOE_SUB), 0)
    c_i = lax.broadcasted_iota(jnp.int32, (MOE_SUB, MOE_SUB), 1)
    earlier_rows = jnp.where(c_i < r_i, 1.0, 0.0).astype(BF16)
    earlier_cols = jnp.where(r_i < c_i, 1.0, 0.0).astype(BF16)
    for s in range(comb.shape[0] // MOE_SUB):
        rows = slice(s * MOE_SUB, (s + 1) * MOE_SUB)
        rank = jnp.dot(earlier_rows, jnp.where(routed[rows], 1.0, 0.0).astype(BF16),
                       preferred_element_type=F32)
        slot_ref[rows, :] = jnp.where(routed[rows], rank, MOE_UNROUTED)
        rank_t = jnp.dot(jnp.where(routed_t[:, rows], 1.0, 0.0).astype(BF16), earlier_cols,
                         preferred_element_type=F32)
        slott_ref[:, rows] = jnp.where(routed_t[:, rows], rank_t, MOE_UNROUTED)


def _out_router(mixed, x, w_out, gain, w_router, tm, mm):
    n = x.shape[0]
    row = lambda w: pl.BlockSpec((tm, w), lambda i: (i, 0))
    const = lambda a: _resident(a.shape, 1)
    return pl.pallas_call(
        functools.partial(_out_router_kernel, mm=mm),
        grid=(pl.cdiv(n, tm),),
        in_specs=[row(D_MODEL), row(D_MODEL), const(w_out), const(gain), const(w_router)],
        out_specs=[row(D_MODEL), row(D_MODEL), row(LANES), row(LANES),
                   pl.BlockSpec((N_EXPERTS, tm), lambda i: (0, i))],
        out_shape=[jax.ShapeDtypeStruct((n, D_MODEL), F32),
                   jax.ShapeDtypeStruct((n, D_MODEL), mm.act),
                   jax.ShapeDtypeStruct((n, LANES), F32),
                   jax.ShapeDtypeStruct((n, LANES), F32),
                   jax.ShapeDtypeStruct((N_EXPERTS, n), F32)],
        compiler_params=pltpu.CompilerParams(dimension_semantics=("parallel",),
                                             vmem_limit_bytes=VMEM_LIMIT),
        name="out_router",
    )(mixed, x, w_out, gain, w_router)


def _moe_kernel(cnt_ref, hn_ref, x1_ref, comb_ref, slot_ref, slott_ref, wg_ref, wu_ref, wd_ref, gf_ref, o_ref,
                acc_ref, xc_ref, y_ref, *, tm, mm):
    i = pl.program_id(0)
    e = pl.program_id(1)
    nsub = tm // MOE_SUB

    @pl.when((i == 0) & (e == 0))
    def _():
        xc_ref[...] = jnp.zeros_like(xc_ref)
        y_ref[...] = jnp.zeros_like(y_ref)

    @pl.when(e == 0)
    def _():
        acc_ref[...] = jnp.zeros_like(acc_ref)

    lane = lax.broadcasted_iota(jnp.int32, (MOE_SUB, LANES), 1)
    expert_row = lax.broadcasted_iota(jnp.int32, (N_EXPERTS, MOE_SUB), 0) == e

    spans = []
    base = jnp.int32(0)
    for s in range(nsub):
        start = (base // MOE_ALIGN) * MOE_ALIGN
        spans.append((pl.multiple_of(start, MOE_ALIGN), (base - start).astype(F32)))
        base = base + cnt_ref[(i * nsub + s) * N_EXPERTS + e]
    total = base

    slot_row = lax.broadcasted_iota(jnp.int32, (MOE_WIN, MOE_SUB), 0).astype(F32)
    for s, (start, shift) in enumerate(spans):
        rows = slice(s * MOE_SUB, (s + 1) * MOE_SUB)
        slot = jnp.sum(jnp.where(expert_row, slott_ref[:, rows], 0.0), axis=0, keepdims=True)
        onehot = jnp.where(slot + shift == slot_row, 1.0, 0.0).astype(BF16)
        packed = mm.dot(onehot, hn_ref[rows, :])
        if s == 0:
            xc_ref[pl.ds(start, MOE_ALIGN), :] = packed[:MOE_ALIGN]
        else:
            xc_ref[pl.ds(start, MOE_ALIGN), :] += packed[:MOE_ALIGN]
        xc_ref[pl.ds(start + MOE_ALIGN, MOE_WIN - MOE_ALIGN), :] = packed[MOE_ALIGN:]

    def ffn(c, carry):
        r = pl.ds(pl.multiple_of(c * MOE_CHUNK, MOE_CHUNK), MOE_CHUNK)
        x = xc_ref[r, :].astype(mm.act)
        hidden = _silu(mm.dot(x, wg_ref[0])) * mm.dot(x, wu_ref[0])
        y_ref[r, :] = mm.dot(hidden, wd_ref[0])
        return carry

    lax.fori_loop(0, (total + MOE_CHUNK - 1) // MOE_CHUNK, ffn, 0)

    slot_lane = lax.broadcasted_iota(jnp.int32, (MOE_SUB, MOE_WIN), 1).astype(F32)
    for s, (start, shift) in enumerate(spans):
        rows = slice(s * MOE_SUB, (s + 1) * MOE_SUB)
        gate = jnp.sum(jnp.where(lane == e, comb_ref[rows, :], 0.0), axis=-1, keepdims=True)
        slot = jnp.sum(jnp.where(lane == e, slot_ref[rows, :], 0.0), axis=-1, keepdims=True)
        onehot = jnp.where(slot + shift == slot_lane, 1.0, 0.0)
        acc_ref[rows, :] += gate * mm.dot(onehot, y_ref[pl.ds(start, MOE_WIN), :])

    @pl.when(e == N_EXPERTS - 1)
    def _():
        o_ref[...] = _rms(x1_ref[...] + acc_ref[...], gf_ref[...])


def _moe(hn, x1, comb, slot, slot_t, wg, wu, wd, gain_final, tm, mm):
    n = x1.shape[0]
    assert n % tm == 0 and tm % MOE_SUB == 0
    routed = comb[:, :N_EXPERTS] > 0.0
    counts = jnp.sum(routed.reshape(n // MOE_SUB, MOE_SUB, N_EXPERTS), axis=1, dtype=jnp.int32).reshape(-1)
    row = lambda w: pl.BlockSpec((tm, w), lambda i, e, cnt: (i, 0))
    expert = lambda a, b: pl.BlockSpec((1, a, b), lambda i, e, cnt: (e, 0, 0))
    packed_rows = tm + MOE_WIN + MOE_CHUNK
    return pl.pallas_call(
        functools.partial(_moe_kernel, tm=tm, mm=mm),
        grid_spec=pltpu.PrefetchScalarGridSpec(
            num_scalar_prefetch=1,
            grid=(n // tm, N_EXPERTS),
            in_specs=[row(D_MODEL), row(D_MODEL), row(LANES), row(LANES),
                      pl.BlockSpec((N_EXPERTS, tm), lambda i, e, cnt: (0, i)),
                      expert(D_MODEL, D_FF_EXPERT), expert(D_MODEL, D_FF_EXPERT), expert(D_FF_EXPERT, D_MODEL),
                      pl.BlockSpec((1, D_MODEL), lambda i, e, cnt: (0, 0))],
            out_specs=row(D_MODEL),
            scratch_shapes=[pltpu.VMEM((tm, D_MODEL), F32),
                            pltpu.VMEM((packed_rows, D_MODEL), F32),
                            pltpu.VMEM((packed_rows, D_MODEL), F32)]),
        out_shape=jax.ShapeDtypeStruct((n, D_MODEL), F32),
        compiler_params=pltpu.CompilerParams(dimension_semantics=("arbitrary", "arbitrary"),
                                             vmem_limit_bytes=VMEM_LIMIT),
        name="moe",
    )(counts, hn, x1, comb, slot, slot_t, wg, wu, wd, gain_final)


def _lane_row(vec, offset):
    return jnp.zeros((1, LANES), F32).at[0, offset:offset + vec.shape[0]].set(vec.astype(F32))


def _block_diag(w, nblk):
    g, a, b = w.shape
    hg = g // nblk
    w = w.reshape(nblk, hg, a, b)
    eye = jnp.eye(hg, dtype=w.dtype)
    return jnp.einsum("sgab,gh->sgahb", w, eye).reshape(nblk, hg * a, hg * b)


MATMUL_WEIGHTS = ("w_in", "bb_re", "bb_im", "cc_re", "cc_im", "w_glu", "w_out")


def _layer_params(l, a):
    w_in = a["w_in"][l]
    w_cat = jnp.concatenate([w_in[:, :QKV_DIM + W_A], w_in[:, QKV_DIM + W_A + 2 * H_A:],
                             w_in[:, QKV_DIM + W_A:QKV_DIM + W_A + 2 * H_A],
                             jnp.zeros((D_MODEL, LANES - 2 * H_A), F32)], axis=1)
    lam_re = a["ssm_a_re"][l]
    lam_im = a["ssm_a_im"][l]
    delta = jnp.exp(a["ssm_log_dt"][l])[:, None]
    mag = jnp.exp(lam_re * delta)
    ab_re = mag * jnp.cos(lam_im * delta)
    ab_im = mag * jnp.sin(lam_im * delta)
    den = lam_re * lam_re + lam_im * lam_im
    f_re = ((ab_re - 1.0) * lam_re + ab_im * lam_im) / den
    f_im = (ab_im * lam_re - (ab_re - 1.0) * lam_im) / den
    b_re = a["ssm_b_re"][l]
    b_im = a["ssm_b_im"][l]
    bb_re = f_re[..., None] * b_re - f_im[..., None] * b_im
    bb_im = f_re[..., None] * b_im + f_im[..., None] * b_re
    return dict(
        norm_mix=a["norm_mix"][l][None], w_in=w_cat,
        conv_w=a["conv_w"][l],
        a_log_row=_lane_row(a["a_log"][l], H_A), dt_bias_row=_lane_row(a["dt_bias"][l], H_A),
        norm_gate=a["norm_gate"][l][None],
        a_re=ab_re.reshape(1, SSM_STATE), a_im=ab_im.reshape(1, SSM_STATE),
        bb_re=_block_diag(jnp.swapaxes(bb_re, 1, 2), SSM_IN_BLOCKS),
        bb_im=_block_diag(jnp.swapaxes(bb_im, 1, 2), SSM_IN_BLOCKS),
        cc_re=_block_diag(jnp.swapaxes(a["ssm_c_re"][l], 1, 2), 2),
        cc_im=_block_diag(jnp.swapaxes(a["ssm_c_im"][l], 1, 2), 2),
        ssm_d=a["ssm_d"][l][None], w_glu=a["w_glu"][l], b_glu=a["b_glu"][l][None],
        norm_ssm=a["norm_ssm"][l][None],
        w_out=a["w_out"][l], norm_ffn=a["norm_ffn"][l][None])


def _trunk(x, states, layers, ffn, moe, l, tm, mm):
    b, t, _ = x.shape
    n = b * t
    xf = x.reshape(n, D_MODEL)
    new_states = []
    for li, p in enumerate(layers):
        s0, c0, hr0, hi0 = states[li]
        mixed, s1, c1, hr1, hi1 = _mixer(xf.reshape(b, t, D_MODEL), s0, c0, hr0.reshape(b, SSM_STATE),
                                         hi0.reshape(b, SSM_STATE), p, l, mm)
        new_states.append((s1, c1, hr1.reshape(b, G_B, P_STATE), hi1.reshape(b, G_B, P_STATE)))
        mixed = mixed.reshape(n, D_MODEL)
        if li == 0:
            xf = _out_ffn(mixed, xf, p["w_out"], p["norm_ffn"], ffn["wg"], ffn["wu"], ffn["wd"], tm, mm)
        else:
            x1, hn, comb, slot, slot_t = _out_router(mixed, xf, p["w_out"], p["norm_ffn"], ffn["router"], tm, mm)
            xf = _moe(hn, x1, comb, slot, slot_t, moe["wg"], moe["wu"], moe["wd"], moe["norm_final"],
                      min(MOE_TM, n), _Matmul(precise=False))
    return xf.reshape(b, t, D_MODEL), new_states


def kernel(x_prompt, x_sample, state_delta, state_conv, state_ssm_re, state_ssm_im, meta_tokens, norm_mix, w_in, conv_w, a_log, dt_bias, norm_gate, ssm_a_re, ssm_a_im, ssm_b_re, ssm_b_im, ssm_c_re, ssm_c_im, ssm_d, ssm_log_dt, w_glu, b_glu, norm_ssm, w_out, norm_ffn, ffn_w_gate, ffn_w_up, ffn_w_down, router_w, moe_w_gate, moe_w_up, moe_w_down, norm_final):
    a = dict(norm_mix=norm_mix, w_in=w_in, conv_w=conv_w, a_log=a_log, dt_bias=dt_bias, norm_gate=norm_gate,
             ssm_a_re=ssm_a_re, ssm_a_im=ssm_a_im, ssm_b_re=ssm_b_re, ssm_b_im=ssm_b_im,
             ssm_c_re=ssm_c_re, ssm_c_im=ssm_c_im, ssm_d=ssm_d, ssm_log_dt=ssm_log_dt,
             w_glu=w_glu, b_glu=b_glu, norm_ssm=norm_ssm, w_out=w_out, norm_ffn=norm_ffn)
    depth = w_in.shape[0]

    layers32 = [_layer_params(l, a) for l in range(depth)]
    ffn32 = dict(wg=ffn_w_gate[0], wu=ffn_w_up[0], wd=ffn_w_down[0],
                 router=jnp.pad(router_w[0], ((0, 0), (0, LANES - N_EXPERTS))))
    moe = dict(wg=moe_w_gate[0].astype(BF16), wu=moe_w_up[0].astype(BF16), wd=moe_w_down[0].astype(BF16),
               norm_final=norm_final[None])

    def weights(mm):
        layers = [{k: (v.astype(mm.act) if k in MATMUL_WEIGHTS else v) for k, v in p.items()} for p in layers32]
        return layers, {k: v.astype(mm.act) for k, v in ffn32.items()}, moe

    bp = x_prompt.shape[0]
    bs, ts = x_sample.shape[0], x_sample.shape[1]
    side_b = 2 * MIX_BT
    pad = side_b - bs - 1
    x_side = jnp.concatenate([x_sample, meta_tokens[None], jnp.zeros((pad, ts, D_MODEL), F32)], axis=0)

    def side_state(st):
        zeros = jnp.zeros((side_b - bs,) + st.shape[1:], F32)
        return jnp.concatenate([st, zeros], axis=0)

    side_states = [(side_state(state_delta[l]), side_state(state_conv[l]),
                    side_state(state_ssm_re[l]), side_state(state_ssm_im[l])) for l in range(depth)]
    precise = _Matmul(precise=True)
    y_side, side_new = _trunk(x_side, side_states, *weights(precise), ts, side_b * ts, precise)

    def from_meta(st):
        return jnp.broadcast_to(st[bs:bs + 1], (bp,) + st.shape[1:])

    main_states = [tuple(from_meta(st) for st in side_new[l]) for l in range(depth)]
    fast = _Matmul(precise=False)
    y_prompt, main_new = _trunk(x_prompt, main_states, *weights(fast), CHUNK, 512, fast)

    def stack(new, idx, count):
        return jnp.stack([new[l][idx][:count] for l in range(depth)])

    return (y_prompt, y_side[:bs],
            stack(main_new, 0, bp), stack(main_new, 1, bp), stack(main_new, 2, bp), stack(main_new, 3, bp),
            stack(side_new, 0, bs), stack(side_new, 1, bs), stack(side_new, 2, bs), stack(side_new, 3, bs))
```

```python
import functools

import jax
import jax.numpy as jnp
from jax import lax
from jax.experimental import pallas as pl
from jax.experimental.pallas import tpu as pltpu

F32 = jnp.float32
BF16 = jnp.bfloat16

D_MODEL = 1024
N_META = 16
H_A = 4
DK = 128
DV = 128
W_A = H_A * DV
K_CONV = 4
QKV_DIM = 2 * H_A * DK + H_A * DV
W_B = D_MODEL - W_A
GROUP_CH = 16
G_B = W_B // GROUP_CH
P_STATE = 64
SSM_STATE = G_B * P_STATE
D_FF = 2816
N_EXPERTS = 8
D_FF_EXPERT = 1408
EPS = 1e-6

LANES = 128
SUBLANES = 8
PROJ_PAD = QKV_DIM + W_A + W_B + LANES
COL_Z = QKV_DIM
COL_U = QKV_DIM + W_A
COL_BA = QKV_DIM + W_A + W_B
CONV_PAD = SUBLANES
SSM_IN_BLOCKS = W_B // LANES
SSM_HALF_STATE = SSM_STATE // 2
MIX_BT = 8
DELTA_STACK = 2
CHUNK = 64
MOE_TM = 1024
MOE_SUB = LANES
MOE_CHUNK = 128
MOE_ALIGN = 2 * SUBLANES
MOE_WIN = MOE_SUB + MOE_ALIGN
MOE_UNROUTED = -2.0 * MOE_WIN
VMEM_LIMIT = 60 * 1024 * 1024


class _Matmul:
    def __init__(self, precise):
        self.precise = precise
        self.act = F32 if precise else BF16

    def _args(self, a, b):
        if self.precise:
            return (a.astype(F32), b.astype(F32)), dict(precision=lax.Precision.HIGHEST)
        return (a.astype(BF16), b.astype(BF16)), {}

    def dot(self, a, b):
        args, kw = self._args(a, b)
        return jnp.dot(*args, preferred_element_type=F32, **kw)

    def einsum(self, spec, a, b):
        args, kw = self._args(a, b)
        return jnp.einsum(spec, *args, preferred_element_type=F32, **kw)


def _sigmoid(x):
    return 1.0 / (1.0 + jnp.exp(-x))


def _silu(x):
    return x * _sigmoid(x)


def _rms(x, gain):
    return x * lax.rsqrt(jnp.mean(x * x, axis=-1, keepdims=True) + EPS) * gain


def _resident(shape, grid_rank):
    zeros = (0,) * len(shape)
    index_map = (lambda i: zeros) if grid_rank == 1 else (lambda i, j: zeros)
    return pl.BlockSpec(shape, index_map, pipeline_mode=pl.Buffered(1))


def _delta_head(q, k, v, beta, gcum, s, mm):
    n, l, _ = q.shape
    ns = DELTA_STACK
    p = n // ns

    def parts(x):
        return [x[i * p:(i + 1) * p] for i in range(ns)]

    def side2(x):
        return jnp.concatenate(parts(x), axis=-1)

    def diag2(x):
        zero = jnp.zeros_like(x[:p])
        return jnp.concatenate([jnp.concatenate([xi if j == i else zero for j in range(ns)], axis=-1)
                                for i, xi in enumerate(parts(x))], axis=1)

    def unside2(x2, w):
        return jnp.concatenate([x2[..., i * w:(i + 1) * w] for i in range(ns)], axis=0)

    lane2 = lax.broadcasted_iota(jnp.int32, (p, l, ns * l), 2)
    block = sum((lane2 >= i * l).astype(jnp.int32) for i in range(1, ns))

    def per_row2(x):
        xs = parts(x)
        out = xs[ns - 1]
        for i in range(ns - 2, -1, -1):
            out = jnp.where(block == i, xs[i], out)
        return out

    def own_block2(x2):
        return jnp.concatenate([jnp.where(block == i, x2, 0.0) for i in range(ns)], axis=1)

    g_hi = gcum.astype(BF16).astype(F32)
    r1 = gcum - g_hi
    g_mid = r1.astype(BF16).astype(F32)
    g_lo = r1 - g_mid
    lane = lax.broadcasted_iota(jnp.int32, (n, l, LANES), 2)
    pieces = jnp.where(lane == 0, g_hi, jnp.where(lane == 1, g_mid, jnp.where(lane == 2, g_lo, 0.0)))
    ones = jnp.ones((p, l, ns * LANES), F32)
    g_row = mm.einsum("bik,bjk->bij", ones, diag2(pieces))
    g_col = per_row2(gcum)
    ii = lax.broadcasted_iota(jnp.int32, (p, l, ns * l), 1)
    jj = lane2 - block * l
    incl = ii >= jj
    strict = ii > jj
    decay = jnp.where(incl, jnp.exp(jnp.where(incl, g_col - g_row, 0.0)), 0.0)
    k_diag = diag2(k)
    kk = mm.einsum("bik,bjk->bij", side2(k), k_diag)
    qk = mm.einsum("bik,bjk->bij", side2(q), k_diag) * decay
    neg_m = jnp.where(strict, -(per_row2(beta) * kk * decay), 0.0)
    eye = jnp.where(ii == jj, 1.0, 0.0)
    t_inv = eye + neg_m
    pw = neg_m
    size = 2
    while size < l:
        pw = mm.einsum("bij,bjk->bik", pw, own_block2(pw))
        t_inv = t_inv + mm.einsum("bij,bjk->bik", t_inv, own_block2(pw))
        size *= 2
    eg = jnp.exp(gcum)
    rhs = jnp.concatenate([beta * v, (beta * eg) * k], axis=-1)
    sol = unside2(mm.einsum("bij,bjd->bid", t_inv, diag2(rhs)), 2 * DV)
    u0 = sol[..., :DV]
    w = sol[..., DV:]
    u = u0 - mm.einsum("bik,bkv->biv", w, s)
    o = mm.einsum("bik,bkv->biv", q * eg, s) + unside2(mm.einsum("bij,bjv->biv", qk, diag2(u)), DV)
    g_last = gcum[:, l - 1:l, :]
    kg = k * jnp.exp(g_last - gcum)
    s_new = s * jnp.exp(g_last) + mm.einsum("bjk,bjv->bkv", kg, u)
    return o, s_new


def _mixer_kernel(xfirst_ref, xnext_ref, gin_ref, win_ref, s0_ref, c0_ref, hr0_ref, hi0_ref,
                  convw_ref, alog_ref, dtb_ref, ngate_ref,
                  are_ref, aim_ref, bbre_ref, bbim_ref, ccre_ref, ccim_ref,
                  d_ref, wglu_ref, bglu_ref, nssm_ref,
                  out_ref, s_ref, c_ref, hr_ref, hi_ref,
                  proj_scr, xp_scr, xr_scr, xi_scr, *, bt, l, lookahead, mm):
    rows = bt * l
    chunk = pl.program_id(1)

    def project(x_ref):
        hn = _rms(x_ref[...].reshape(rows, D_MODEL), gin_ref[...])
        return mm.dot(hn, win_ref[...])

    @pl.when(chunk == 0)
    def _():
        s_ref[...] = s0_ref[...]
        hr_ref[...] = hr0_ref[...]
        hi_ref[...] = hi0_ref[...]
        xp_scr[:, CONV_PAD - (K_CONV - 1):CONV_PAD, :] = c0_ref[...]
        proj_scr[0:rows, :] = project(xfirst_ref)

    slot = chunk % 2
    cur = pl.ds(pl.multiple_of(slot * rows, rows), rows)

    def proj_cols(start, width):
        return proj_scr[cur, start:start + width].reshape(bt, l, width)

    xp_scr[:, CONV_PAD:CONV_PAD + l, :] = proj_cols(0, QKV_DIM)
    xp = xp_scr[...]
    acc = xp[:, CONV_PAD:, :] * convw_ref[K_CONV - 1:K_CONV, :]
    for back in range(1, K_CONV):
        shifted = pltpu.roll(xp, back, 1)[:, CONV_PAD:, :]
        acc = acc + shifted * convw_ref[K_CONV - 1 - back:K_CONV - back, :]
    tail = xp_scr[:, CONV_PAD + l - (K_CONV - 1):CONV_PAD + l, :]
    xp_scr[:, CONV_PAD - (K_CONV - 1):CONV_PAD, :] = tail
    c_ref[...] = tail
    qkv = _silu(acc)

    ba = proj_scr[cur, COL_BA:COL_BA + LANES]
    beta_all = _sigmoid(ba)
    sp_in = ba + dtb_ref[...]
    softplus = jnp.maximum(sp_in, 0.0) + jnp.log1p(jnp.exp(-jnp.abs(sp_in)))
    g_all = -jnp.exp(alog_ref[...]) * softplus
    t_idx = lax.broadcasted_iota(jnp.int32, (rows, LANES), 0) % l
    shift = 1
    while shift < l:
        g_all = g_all + jnp.where(t_idx >= shift, pltpu.roll(g_all, shift, 0), 0.0)
        shift *= 2
    beta_all = beta_all.reshape(bt, l, LANES)
    g_all = g_all.reshape(bt, l, LANES)

    def heads(first_col, width, src):
        return jnp.concatenate([src[:, :, first_col + h * width:first_col + (h + 1) * width]
                                for h in range(H_A)], axis=0)

    z = proj_cols(COL_Z, W_A)
    q = heads(0, DK, qkv)
    k = heads(H_A * DK, DK, qkv)
    v = heads(2 * H_A * DK, DV, qkv)
    q = q * lax.rsqrt(jnp.sum(q * q, axis=-1, keepdims=True) + EPS) * (DK ** -0.5)
    k = k * lax.rsqrt(jnp.sum(k * k, axis=-1, keepdims=True) + EPS)
    s_all = jnp.concatenate([s_ref[:, h] for h in range(H_A)], axis=0)
    o, s_new = _delta_head(q, k, v, heads(0, 1, beta_all), heads(H_A, 1, g_all), s_all, mm)
    o = _rms(o, ngate_ref[...]) * _silu(heads(0, DV, z))
    for h in range(H_A):
        s_ref[:, h] = s_new[h * bt:(h + 1) * bt]
        out_ref[:, :, h * DV:(h + 1) * DV] = o[h * bt:(h + 1) * bt].astype(out_ref.dtype)

    u_tb = pltpu.einshape("blc->lbc", proj_cols(COL_U, W_B)).reshape(rows, W_B)
    in_w = W_B // SSM_IN_BLOCKS
    st_w = SSM_STATE // SSM_IN_BLOCKS
    for blk in range(SSM_IN_BLOCKS):
        ub = u_tb[:, blk * in_w:(blk + 1) * in_w]
        cols = slice(blk * st_w, (blk + 1) * st_w)
        xr_scr[:, cols] = mm.dot(ub, bbre_ref[blk])
        xi_scr[:, cols] = mm.dot(ub, bbim_ref[blk])

    for half in range(2):
        cols = slice(half * SSM_HALF_STATE, (half + 1) * SSM_HALF_STATE)
        a_re = jnp.broadcast_to(are_ref[:, cols], (bt, SSM_HALF_STATE))
        a_im = jnp.broadcast_to(aim_ref[:, cols], (bt, SSM_HALF_STATE))

        h_re, h_im = hr_ref[:, cols], hi_ref[:, cols]
        for t in range(l):
            step_rows = slice(t * bt, (t + 1) * bt)
            h_re, h_im = (a_re * h_re - a_im * h_im + xr_scr[step_rows, cols],
                          a_re * h_im + a_im * h_re + xi_scr[step_rows, cols])
            xr_scr[step_rows, cols] = h_re
            xi_scr[step_rows, cols] = h_im
        hr_ref[:, cols] = h_re
        hi_ref[:, cols] = h_im

    y_parts = []
    for half in range(2):
        cols = slice(half * SSM_HALF_STATE, (half + 1) * SSM_HALF_STATE)
        y_parts.append(mm.dot(xr_scr[:, cols], ccre_ref[half]) - mm.dot(xi_scr[:, cols], ccim_ref[half]))
    y = jnp.concatenate(y_parts, axis=-1) + d_ref[...] * u_tb
    gy = 0.5 * y * (1.0 + jnp.tanh(0.7978845608028654 * (y + 0.044715 * (y * y * y))))
    ob = gy * _sigmoid(mm.dot(gy, wglu_ref[...]) + bglu_ref[...])
    ob = _rms(ob, nssm_ref[...])
    out_ref[:, :, W_A:] = pltpu.einshape("lbc->blc", ob.reshape(l, bt, W_B)).astype(out_ref.dtype)

    if lookahead:
        proj_scr[pl.ds(pl.multiple_of((1 - slot) * rows, rows), rows), :] = project(xnext_ref)


def _mixer(x, s0, c0, hr0, hi0, p, l, mm):
    b, t, _ = x.shape
    bt = MIX_BT
    rows = bt * l
    nchunks = t // l
    grid = (b // bt, nchunks)

    def state(shape, **kw):
        nd = len(shape)
        return pl.BlockSpec((bt,) + shape, lambda i, n: (i,) + (0,) * nd, **kw)

    def const(shape):
        return _resident(shape, 2)

    x_first = pl.BlockSpec((bt, l, D_MODEL), lambda i, n: (i, 0, 0), pipeline_mode=pl.Buffered(1))
    x_next = pl.BlockSpec((bt, l, D_MODEL), lambda i, n: (i, jnp.minimum(n + 1, nchunks - 1), 0))
    consts = [p["norm_mix"], p["w_in"]]
    consts_tail = [p["conv_w"], p["a_log_row"], p["dt_bias_row"], p["norm_gate"],
              p["a_re"], p["a_im"], p["bb_re"], p["bb_im"], p["cc_re"], p["cc_im"],
              p["ssm_d"], p["w_glu"], p["b_glu"], p["norm_ssm"]]
    state_shapes = [(H_A, DK, DV), (K_CONV - 1, QKV_DIM), (SSM_STATE,), (SSM_STATE,)]
    out_shapes = ([jax.ShapeDtypeStruct((b, t, D_MODEL), mm.act)]
                  + [jax.ShapeDtypeStruct((b,) + s, F32) for s in state_shapes])
    return pl.pallas_call(
        functools.partial(_mixer_kernel, bt=bt, l=l, lookahead=nchunks > 1, mm=mm),
        grid=grid,
        in_specs=([x_first, x_next] + [const(c.shape) for c in consts]
                  + [state(s, pipeline_mode=pl.Buffered(1)) for s in state_shapes]
                  + [const(c.shape) for c in consts_tail]),
        out_specs=[pl.BlockSpec((bt, l, D_MODEL), lambda i, n: (i, n, 0))] + [state(s) for s in state_shapes],
        out_shape=out_shapes,
        scratch_shapes=[pltpu.VMEM((2 * rows, PROJ_PAD), F32),
                        pltpu.VMEM((bt, CONV_PAD + l, QKV_DIM), F32),
                        pltpu.VMEM((rows, SSM_STATE), F32),
                        pltpu.VMEM((rows, SSM_STATE), F32)],
        compiler_params=pltpu.CompilerParams(dimension_semantics=("parallel", "arbitrary"),
                                             vmem_limit_bytes=VMEM_LIMIT),
        name="mixer",
    )(x, x, *consts, s0, c0, hr0, hi0, *consts_tail)


def _out_ffn_kernel(mix_ref, x_ref, wo_ref, g_ref, wg_ref, wu_ref, wd_ref, o_ref, *, mm):
    x1 = x_ref[...] + mm.dot(mix_ref[...], wo_ref[...])
    hn = _rms(x1, g_ref[...]).astype(mm.act)
    acc = x1
    for c in range(D_FF // D_FF_EXPERT):
        cols = slice(c * D_FF_EXPERT, (c + 1) * D_FF_EXPERT)
        hidden = _silu(mm.dot(hn, wg_ref[:, cols])) * mm.dot(hn, wu_ref[:, cols])
        acc = acc + mm.dot(hidden, wd_ref[cols, :])
    o_ref[...] = acc


def _out_ffn(mixed, x, w_out, gain, wg, wu, wd, tm, mm):
    n = x.shape[0]
    row = lambda w: pl.BlockSpec((tm, w), lambda i: (i, 0))
    const = lambda a: _resident(a.shape, 1)
    return pl.pallas_call(
        functools.partial(_out_ffn_kernel, mm=mm),
        grid=(pl.cdiv(n, tm),),
        in_specs=[row(D_MODEL), row(D_MODEL), const(w_out), const(gain), const(wg), const(wu), const(wd)],
        out_specs=row(D_MODEL),
        out_shape=jax.ShapeDtypeStruct((n, D_MODEL), F32),
        compiler_params=pltpu.CompilerParams(dimension_semantics=("parallel",),
                                             vmem_limit_bytes=VMEM_LIMIT),
        name="out_ffn",
    )(mixed, x, w_out, gain, wg, wu, wd)


def _out_router_kernel(mix_ref, x_ref, wo_ref, g_ref, wr_ref, x1_ref, hn_ref, comb_ref, slot_ref, slott_ref,
                       *, mm):
    x1 = x_ref[...] + mm.dot(mix_ref[...], wo_ref[...])
    x1_ref[...] = x1
    hn = _rms(x1, g_ref[...]).astype(mm.act)
    hn_ref[...] = hn
    logits = mm.dot(hn, wr_ref[...])
    lane = lax.broadcasted_iota(jnp.int32, logits.shape, 1).astype(F32)
    neg = jnp.float32(-jnp.inf)
    logits = jnp.where(lane < N_EXPERTS, logits, neg)
    v1 = jnp.max(logits, axis=-1, keepdims=True)
    i1 = jnp.min(jnp.where(logits == v1, lane, float(LANES)), axis=-1, keepdims=True)
    rest = jnp.where(lane == i1, neg, logits)
    v2 = jnp.max(rest, axis=-1, keepdims=True)
    i2 = jnp.min(jnp.where(rest == v2, lane, float(LANES)), axis=-1, keepdims=True)
    e2 = jnp.exp(v2 - v1)
    den = 1.0 + e2
    comb = jnp.where(lane == i1, 1.0 / den, jnp.where(lane == i2, e2 / den, 0.0))
    comb_ref[...] = comb
    routed = comb > 0.0
    routed_t = routed.astype(F32).T[:N_EXPERTS, :] > 0.0
    r_i = lax.broadcasted_iota(jnp.int32, (MOE_SUB, MOE_SUB), 0)
    c_i = lax.broadcasted_iota(jnp.int32, (MOE_SUB, MOE_SUB), 1)
    earlier_rows = jnp.where(c_i < r_i, 1.0, 0.0).astype(BF16)
    earlier_cols = jnp.where(r_i < c_i, 1.0, 0.0).astype(BF16)
    for s in range(comb.shape[0] // MOE_SUB):
        rows = slice(s * MOE_SUB, (s + 1) * MOE_SUB)
        rank = jnp.dot(earlier_rows, jnp.where(routed[rows], 1.0, 0.0).astype(BF16),
                       preferred_element_type=F32)
        slot_ref[rows, :] = jnp.where(routed[rows], rank, MOE_UNROUTED)
        rank_t = jnp.dot(jnp.where(routed_t[:, rows], 1.0, 0.0).astype(BF16), earlier_cols,
                         preferred_element_type=F32)
        slott_ref[:, rows] = jnp.where(routed_t[:, rows], rank_t, MOE_UNROUTED)


def _out_router(mixed, x, w_out, gain, w_router, tm, mm):
    n = x.shape[0]
    row = lambda w: pl.BlockSpec((tm, w), lambda i: (i, 0))
    const = lambda a: _resident(a.shape, 1)
    return pl.pallas_call(
        functools.partial(_out_router_kernel, mm=mm),
        grid=(pl.cdiv(n, tm),),
        in_specs=[row(D_MODEL), row(D_MODEL), const(w_out), const(gain), const(w_router)],
        out_specs=[row(D_MODEL), row(D_MODEL), row(LANES), row(LANES),
                   pl.BlockSpec((N_EXPERTS, tm), lambda i: (0, i))],
        out_shape=[jax.ShapeDtypeStruct((n, D_MODEL), F32),
                   jax.ShapeDtypeStruct((n, D_MODEL), mm.act),
                   jax.ShapeDtypeStruct((n, LANES), F32),
                   jax.ShapeDtypeStruct((n, LANES), F32),
                   jax.ShapeDtypeStruct((N_EXPERTS, n), F32)],
        compiler_params=pltpu.CompilerParams(dimension_semantics=("parallel",),
                                             vmem_limit_bytes=VMEM_LIMIT),
        name="out_router",
    )(mixed, x, w_out, gain, w_router)


def _moe_kernel(cnt_ref, hn_ref, x1_ref, comb_ref, slot_ref, slott_ref, wg_ref, wu_ref, wd_ref, gf_ref, o_ref,
                acc_ref, xc_ref, y_ref, *, tm, mm):
    i = pl.program_id(0)
    e = pl.program_id(1)
    nsub = tm // MOE_SUB

    @pl.when((i == 0) & (e == 0))
    def _():
        xc_ref[...] = jnp.zeros_like(xc_ref)
        y_ref[...] = jnp.zeros_like(y_ref)

    @pl.when(e == 0)
    def _():
        acc_ref[...] = jnp.zeros_like(acc_ref)

    lane = lax.broadcasted_iota(jnp.int32, (MOE_SUB, LANES), 1)
    expert_row = lax.broadcasted_iota(jnp.int32, (N_EXPERTS, MOE_SUB), 0) == e

    spans = []
    base = jnp.int32(0)
    for s in range(nsub):
        start = (base // MOE_ALIGN) * MOE_ALIGN
        spans.append((pl.multiple_of(start, MOE_ALIGN), (base - start).astype(F32)))
        base = base + cnt_ref[(i * nsub + s) * N_EXPERTS + e]
    total = base

    slot_row = lax.broadcasted_iota(jnp.int32, (MOE_WIN, MOE_SUB), 0).astype(F32)
    for s, (start, shift) in enumerate(spans):
        rows = slice(s * MOE_SUB, (s + 1) * MOE_SUB)
        slot = jnp.sum(jnp.where(expert_row, slott_ref[:, rows], 0.0), axis=0, keepdims=True)
        onehot = jnp.where(slot + shift == slot_row, 1.0, 0.0).astype(BF16)
        packed = mm.dot(onehot, hn_ref[rows, :])
        head = packed[:MOE_ALIGN]
        if s > 0:
            head = head + xc_ref[pl.ds(start, MOE_ALIGN), :].astype(F32)
        xc_ref[pl.ds(start, MOE_ALIGN), :] = head.astype(xc_ref.dtype)
        xc_ref[pl.ds(start + MOE_ALIGN, MOE_WIN - MOE_ALIGN), :] = packed[MOE_ALIGN:].astype(xc_ref.dtype)

    def ffn(c, carry):
        r = pl.ds(pl.multiple_of(c * MOE_CHUNK, MOE_CHUNK), MOE_CHUNK)
        x = xc_ref[r, :].astype(mm.act)
        hidden = _silu(mm.dot(x, wg_ref[0])) * mm.dot(x, wu_ref[0])
        y_ref[r, :] = mm.dot(hidden, wd_ref[0])
        return carry

    lax.fori_loop(0, (total + MOE_CHUNK - 1) // MOE_CHUNK, ffn, 0)

    slot_lane = lax.broadcasted_iota(jnp.int32, (MOE_SUB, MOE_WIN), 1).astype(F32)
    for s, (start, shift) in enumerate(spans):
        rows = slice(s * MOE_SUB, (s + 1) * MOE_SUB)
        gate = jnp.sum(jnp.where(lane == e, comb_ref[rows, :], 0.0), axis=-1, keepdims=True)
        slot = jnp.sum(jnp.where(lane == e, slot_ref[rows, :], 0.0), axis=-1, keepdims=True)
        onehot = jnp.where(slot + shift == slot_lane, 1.0, 0.0)
        acc_ref[rows, :] += gate * mm.dot(onehot, y_ref[pl.ds(start, MOE_WIN), :])

    @pl.when(e == N_EXPERTS - 1)
    def _():
        o_ref[...] = _rms(x1_ref[...] + acc_ref[...], gf_ref[...])


def _moe(hn, x1, comb, slot, slot_t, wg, wu, wd, gain_final, tm, mm):
    n = x1.shape[0]
    assert n % tm == 0 and tm % MOE_SUB == 0
    routed = comb[:, :N_EXPERTS] > 0.0
    counts = jnp.sum(routed.reshape(n // MOE_SUB, MOE_SUB, N_EXPERTS), axis=1, dtype=jnp.int32).reshape(-1)
    row = lambda w: pl.BlockSpec((tm, w), lambda i, e, cnt: (i, 0))
    expert = lambda a, b: pl.BlockSpec((1, a, b), lambda i, e, cnt: (e, 0, 0))
    packed_rows = tm + MOE_WIN + MOE_CHUNK
    return pl.pallas_call(
        functools.partial(_moe_kernel, tm=tm, mm=mm),
        grid_spec=pltpu.PrefetchScalarGridSpec(
            num_scalar_prefetch=1,
            grid=(n // tm, N_EXPERTS),
            in_specs=[row(D_MODEL), row(D_MODEL), row(LANES), row(LANES),
                      pl.BlockSpec((N_EXPERTS, tm), lambda i, e, cnt: (0, i)),
                      expert(D_MODEL, D_FF_EXPERT), expert(D_MODEL, D_FF_EXPERT), expert(D_FF_EXPERT, D_MODEL),
                      pl.BlockSpec((1, D_MODEL), lambda i, e, cnt: (0, 0))],
            out_specs=row(D_MODEL),
            scratch_shapes=[pltpu.VMEM((tm, D_MODEL), F32),
                            pltpu.VMEM((packed_rows, D_MODEL), BF16),
                            pltpu.VMEM((packed_rows, D_MODEL), F32)]),
        out_shape=jax.ShapeDtypeStruct((n, D_MODEL), F32),
        compiler_params=pltpu.CompilerParams(dimension_semantics=("arbitrary", "arbitrary"),
                                             vmem_limit_bytes=VMEM_LIMIT),
        name="moe",
    )(counts, hn, x1, comb, slot, slot_t, wg, wu, wd, gain_final)


def _lane_row(vec, offset):
    return jnp.zeros((1, LANES), F32).at[0, offset:offset + vec.shape[0]].set(vec.astype(F32))


def _block_diag(w, nblk):
    g, a, b = w.shape
    hg = g // nblk
    w = w.reshape(nblk, hg, a, b)
    eye = jnp.eye(hg, dtype=w.dtype)
    return jnp.einsum("sgab,gh->sgahb", w, eye).reshape(nblk, hg * a, hg * b)


MATMUL_WEIGHTS = ("w_in", "bb_re", "bb_im", "cc_re", "cc_im", "w_glu", "w_out")


def _layer_params(l, a):
    w_in = a["w_in"][l]
    w_cat = jnp.concatenate([w_in[:, :QKV_DIM + W_A], w_in[:, QKV_DIM + W_A + 2 * H_A:],
                             w_in[:, QKV_DIM + W_A:QKV_DIM + W_A + 2 * H_A],
                             jnp.zeros((D_MODEL, LANES - 2 * H_A), F32)], axis=1)
    lam_re = a["ssm_a_re"][l]
    lam_im = a["ssm_a_im"][l]
    delta = jnp.exp(a["ssm_log_dt"][l])[:, None]
    mag = jnp.exp(lam_re * delta)
    ab_re = mag * jnp.cos(lam_im * delta)
    ab_im = mag * jnp.sin(lam_im * delta)
    den = lam_re * lam_re + lam_im * lam_im
    f_re = ((ab_re - 1.0) * lam_re + ab_im * lam_im) / den
    f_im = (ab_im * lam_re - (ab_re - 1.0) * lam_im) / den
    b_re = a["ssm_b_re"][l]
    b_im = a["ssm_b_im"][l]
    bb_re = f_re[..., None] * b_re - f_im[..., None] * b_im
    bb_im = f_re[..., None] * b_im + f_im[..., None] * b_re
    return dict(
        norm_mix=a["norm_mix"][l][None], w_in=w_cat,
        conv_w=a["conv_w"][l],
        a_log_row=_lane_row(a["a_log"][l], H_A), dt_bias_row=_lane_row(a["dt_bias"][l], H_A),
        norm_gate=a["norm_gate"][l][None],
        a_re=ab_re.reshape(1, SSM_STATE), a_im=ab_im.reshape(1, SSM_STATE),
        bb_re=_block_diag(jnp.swapaxes(bb_re, 1, 2), SSM_IN_BLOCKS),
        bb_im=_block_diag(jnp.swapaxes(bb_im, 1, 2), SSM_IN_BLOCKS),
        cc_re=_block_diag(jnp.swapaxes(a["ssm_c_re"][l], 1, 2), 2),
        cc_im=_block_diag(jnp.swapaxes(a["ssm_c_im"][l], 1, 2), 2),
        ssm_d=a["ssm_d"][l][None], w_glu=a["w_glu"][l], b_glu=a["b_glu"][l][None],
        norm_ssm=a["norm_ssm"][l][None],
        w_out=a["w_out"][l], norm_ffn=a["norm_ffn"][l][None])


def _trunk(x, states, layers, ffn, moe, l, tm, mm):
    b, t, _ = x.shape
    n = b * t
    xf = x.reshape(n, D_MODEL)
    new_states = []
    for li, p in enumerate(layers):
        s0, c0, hr0, hi0 = states[li]
        mixed, s1, c1, hr1, hi1 = _mixer(xf.reshape(b, t, D_MODEL), s0, c0, hr0.reshape(b, SSM_STATE),
                                         hi0.reshape(b, SSM_STATE), p, l, mm)
        new_states.append((s1, c1, hr1.reshape(b, G_B, P_STATE), hi1.reshape(b, G_B, P_STATE)))
        mixed = mixed.reshape(n, D_MODEL)
        if li == 0:
            xf = _out_ffn(mixed, xf, p["w_out"], p["norm_ffn"], ffn["wg"], ffn["wu"], ffn["wd"], tm, mm)
        else:
            x1, hn, comb, slot, slot_t = _out_router(mixed, xf, p["w_out"], p["norm_ffn"], ffn["router"], tm, mm)
            xf = _moe(hn, x1, comb, slot, slot_t, moe["wg"], moe["wu"], moe["wd"], moe["norm_final"],
                      min(MOE_TM, n), _Matmul(precise=False))
    return xf.reshape(b, t, D_MODEL), new_states


def kernel(x_prompt, x_sample, state_delta, state_conv, state_ssm_re, state_ssm_im, meta_tokens, norm_mix, w_in, conv_w, a_log, dt_bias, norm_gate, ssm_a_re, ssm_a_im, ssm_b_re, ssm_b_im, ssm_c_re, ssm_c_im, ssm_d, ssm_log_dt, w_glu, b_glu, norm_ssm, w_out, norm_ffn, ffn_w_gate, ffn_w_up, ffn_w_down, router_w, moe_w_gate, moe_w_up, moe_w_down, norm_final):
    a = dict(norm_mix=norm_mix, w_in=w_in, conv_w=conv_w, a_log=a_log, dt_bias=dt_bias, norm_gate=norm_gate,
             ssm_a_re=ssm_a_re, ssm_a_im=ssm_a_im, ssm_b_re=ssm_b_re, ssm_b_im=ssm_b_im,
             ssm_c_re=ssm_c_re, ssm_c_im=ssm_c_im, ssm_d=ssm_d, ssm_log_dt=ssm_log_dt,
             w_glu=w_glu, b_glu=b_glu, norm_ssm=norm_ssm, w_out=w_out, norm_ffn=norm_ffn)
    depth = w_in.shape[0]

    layers32 = [_layer_params(l, a) for l in range(depth)]
    ffn32 = dict(wg=ffn_w_gate[0], wu=ffn_w_up[0], wd=ffn_w_down[0],
                 router=jnp.pad(router_w[0], ((0, 0), (0, LANES - N_EXPERTS))))
    moe = dict(wg=moe_w_gate[0].astype(BF16), wu=moe_w_up[0].astype(BF16), wd=moe_w_down[0].astype(BF16),
               norm_final=norm_final[None])

    def weights(mm):
        layers = [{k: (v.astype(mm.act) if k in MATMUL_WEIGHTS else v) for k, v in p.items()} for p in layers32]
        return layers, {k: v.astype(mm.act) for k, v in ffn32.items()}, moe

    bp = x_prompt.shape[0]
    bs, ts = x_sample.shape[0], x_sample.shape[1]
    side_b = 2 * MIX_BT
    pad = side_b - bs - 1
    x_side = jnp.concatenate([x_sample, meta_tokens[None], jnp.zeros((pad, ts, D_MODEL), F32)], axis=0)

    def side_state(st):
        zeros = jnp.zeros((side_b - bs,) + st.shape[1:], F32)
        return jnp.concatenate([st, zeros], axis=0)

    side_states = [(side_state(state_delta[l]), side_state(state_conv[l]),
                    side_state(state_ssm_re[l]), side_state(state_ssm_im[l])) for l in range(depth)]
    precise = _Matmul(precise=True)
    y_side, side_new = _trunk(x_side, side_states, *weights(precise), ts, side_b * ts, precise)

    def from_meta(st):
        return jnp.broadcast_to(st[bs:bs + 1], (bp,) + st.shape[1:])

    main_states = [tuple(from_meta(st) for st in side_new[l]) for l in range(depth)]
    fast = _Matmul(precise=False)
    y_prompt, main_new = _trunk(x_prompt, main_states, *weights(fast), CHUNK, 512, fast)

    def stack(new, idx, count):
        return jnp.stack([new[l][idx][:count] for l in range(depth)])

    return (y_prompt, y_side[:bs],
            stack(main_new, 0, bp), stack(main_new, 1, bp), stack(main_new, 2, bp), stack(main_new, 3, bp),
            stack(side_new, 0, bs), stack(side_new, 1, bs), stack(side_new, 2, bs), stack(side_new, 3, bs))
```
